```python
import math
import jax, jax.numpy as jnp
from jax import lax
import numpy as np

D_MODEL = 2048
BATCH = 2
SEQ = 4096
DEPTH = 1

CHUNK = 64
N_MEM = 256

POOL_WINDOWS = (2, 4, 8, 16)
POOL_GROUPS = len(POOL_WINDOWS)
POOL_WIDTH = D_MODEL
POOL_GROUP_DIM = POOL_WIDTH // POOL_GROUPS

SSD_EXPAND = 2
SSD_INNER = SSD_EXPAND * D_MODEL
SSD_HEAD_DIM = 64
SSD_HEADS = SSD_INNER // SSD_HEAD_DIM
SSD_GROUPS = 8
SSD_STATE = 128
SSD_CONV = 4
SSD_CONV_DIM = SSD_INNER + 2 * SSD_GROUPS * SSD_STATE

N_BRANCH = 2
IN_COLS = POOL_WIDTH + SSD_INNER + SSD_CONV_DIM + SSD_HEADS + N_BRANCH * D_MODEL

X_HEADS = 4
X_HEAD_DIM = 128
X_WIDTH = X_HEADS * X_HEAD_DIM

D_FF = 5632

EPS = 1e-6

kernel_name = "hybrid_pool_ssd_macaron_sandwich_block"


def rms_norm(x, g):
    xf = x.astype(jnp.float32)
    y = xf * lax.rsqrt(jnp.mean(xf * xf, axis=-1, keepdims=True) + EPS)
    return (y * g.astype(jnp.float32)).astype(x.dtype)


def swiglu(u, w1, w3, w2):
    return (jax.nn.silu(u @ w1) * (u @ w3)) @ w2


def pool_mixer(p, pool_w, pool_scale):
    b, s, _ = p.shape
    pg = p.reshape(b, s, POOL_GROUPS, POOL_GROUP_DIM)
    cs = jnp.cumsum(pg.astype(jnp.float32), axis=1)
    cs0 = jnp.concatenate([jnp.zeros_like(cs[:, :1]), cs], axis=1)
    t = jnp.arange(s)
    outs = []
    for gi, w in enumerate(POOL_WINDOWS):
        lo = jnp.maximum(t + 1 - w, 0)
        win = cs0[:, 1:, gi] - cs0[:, lo, gi]
        cnt = (t + 1 - lo).astype(jnp.float32)[None, :, None]
        outs.append(win / cnt - pg[:, :, gi].astype(jnp.float32))
    pooled = jnp.stack(outs, axis=2).astype(p.dtype)
    mixed = jnp.einsum('bsgc,gcd->bsgd', pooled, pool_w).reshape(b, s, POOL_WIDTH)
    return mixed * pool_scale


def causal_depthwise_conv(u, w, bias):
    out = lax.conv_general_dilated(
        u, w[:, None, :].astype(u.dtype), window_strides=(1,), padding=[(SSD_CONV - 1, 0)],
        dimension_numbers=('NWC', 'WIO', 'NWC'), feature_group_count=u.shape[-1])
    return out + bias


def ssd_scan(xh, dt, A, Bm, Cm):
    b, s, h, p = xh.shape
    g, n = Bm.shape[2], Bm.shape[3]
    r = h // g
    c = s // CHUNK
    X = (xh * dt[..., None]).reshape(b, c, CHUNK, g, r, p)
    a = (dt * A).reshape(b, c, CHUNK, g, r)
    a_cs = jnp.cumsum(a, axis=2)
    Bc = Bm.reshape(b, c, CHUNK, g, n)
    Cc = Cm.reshape(b, c, CHUNK, g, n)
    seg = a_cs[:, :, :, None] - a_cs[:, :, None, :]
    mask = jnp.tril(jnp.ones((CHUNK, CHUNK), dtype=bool))[None, None, :, :, None, None]
    Lmat = jnp.exp(jnp.where(mask, seg, -jnp.inf))
    CB = jnp.einsum('bclgn,bcsgn->bclsg', Cc, Bc)
    y_diag = jnp.einsum('bclsg,bclsgr,bcsgrp->bclgrp', CB, Lmat, X)
    decay_states = jnp.exp(a_cs[:, :, -1:] - a_cs)
    states = jnp.einsum('bclgn,bclgr,bclgrp->bcgrpn', Bc, decay_states, X)
    chunk_decay = jnp.exp(a_cs[:, :, -1])

    def step(carry, inp):
        st, dec = inp
        return carry * dec[..., None, None] + st, carry

    init = jnp.zeros((b, g, r, p, n), dtype=states.dtype)
    _, prev = lax.scan(step, init, (jnp.moveaxis(states, 1, 0), jnp.moveaxis(chunk_decay, 1, 0)))
    prev = jnp.moveaxis(prev, 0, 1)
    y_off = jnp.einsum('bclgn,bcgrpn,bclgr->bclgrp', Cc, prev, jnp.exp(a_cs))
    return (y_diag + y_off).reshape(b, s, h, p)


def ssd_mixer(z, xbc, dt_raw, conv_w, conv_b, dt_bias, a_log, d_skip, norm_g):
    b, s, _ = z.shape
    xbc = jax.nn.silu(causal_depthwise_conv(xbc, conv_w, conv_b))
    xs = xbc[..., :SSD_INNER].reshape(b, s, SSD_HEADS, SSD_HEAD_DIM)
    Bm = xbc[..., SSD_INNER:SSD_INNER + SSD_GROUPS * SSD_STATE].reshape(b, s, SSD_GROUPS, SSD_STATE)
    Cm = xbc[..., SSD_INNER + SSD_GROUPS * SSD_STATE:].reshape(b, s, SSD_GROUPS, SSD_STATE)
    dt = jax.nn.softplus(dt_raw.astype(jnp.float32) + dt_bias.astype(jnp.float32))
    A = -jnp.exp(a_log.astype(jnp.float32))
    y = ssd_scan(xs, dt, A, Bm, Cm) + d_skip.astype(jnp.float32)[:, None] * xs.astype(jnp.float32)
    y = y.reshape(b, s, SSD_INNER).astype(z.dtype) * jax.nn.silu(z)
    y = rms_norm(y.reshape(b, s, SSD_GROUPS, SSD_INNER // SSD_GROUPS),
                 norm_g.reshape(SSD_GROUPS, SSD_INNER // SSD_GROUPS))
    return y.reshape(b, s, SSD_INNER)


def memory_cross_attention(u, mem_n, w_q, w_k, w_v, w_o):
    b, s, _ = u.shape
    q = (u @ w_q).reshape(b, s, X_HEADS, X_HEAD_DIM)
    k = (mem_n @ w_k).reshape(b, N_MEM, X_HEADS, X_HEAD_DIM)
    v = (mem_n @ w_v).reshape(b, N_MEM, X_HEADS, X_HEAD_DIM)
    sc = jnp.einsum('bshd,bmhd->bhsm', q, k).astype(jnp.float32) / math.sqrt(X_HEAD_DIM)
    pr = jax.nn.softmax(sc, axis=-1).astype(v.dtype)
    o = jnp.einsum('bhsm,bmhd->bshd', pr, v).reshape(b, s, X_WIDTH)
    return o @ w_o


def setup_inputs(seed: int = 0) -> dict:
    key = jax.random.key(seed)
    ks = iter(jax.random.split(key, 64))
    f32 = jnp.float32
    L = DEPTH

    def nrm(shape, fan_in):
        return jax.random.normal(next(ks), shape, f32) * (fan_in ** -0.5)

    def gain(shape):
        return 1.0 + 0.02 * jax.random.normal(next(ks), shape, f32)

    def small(shape):
        return 0.02 * jax.random.normal(next(ks), shape, f32)

    x = jax.random.normal(next(ks), (BATCH, SEQ, D_MODEL), f32)
    mem = jax.random.normal(next(ks), (BATCH, N_MEM, D_MODEL), f32)
    dt0 = jnp.exp(jax.random.uniform(next(ks), (L, SSD_HEADS), f32) * (math.log(0.1) - math.log(0.001))
                  + math.log(0.001))
    dt0 = jnp.maximum(dt0, 1e-4)
    dt_bias = dt0 + jnp.log(-jnp.expm1(-dt0))
    a_log = jnp.log(jax.random.uniform(next(ks), (L, SSD_HEADS), f32, 1.0, 16.0))
    return {
        "x": x,
        "mem": mem,
        "ffn1_pre_g": gain((L, D_MODEL)),
        "ffn1_w1": nrm((L, D_MODEL, D_FF), D_MODEL),
        "ffn1_w3": nrm((L, D_MODEL, D_FF), D_MODEL),
        "ffn1_w2": nrm((L, D_FF, D_MODEL), D_FF),
        "ffn1_post_g": gain((L, D_MODEL)),
        "mix_pre_g": gain((L, D_MODEL)),
        "w_in": nrm((L, D_MODEL, IN_COLS), D_MODEL),
        "b_gate": small((L, N_BRANCH * D_MODEL)),
        "pool_w": nrm((L, POOL_GROUPS, POOL_GROUP_DIM, POOL_GROUP_DIM), POOL_GROUP_DIM),
        "pool_scale": gain((L, POOL_WIDTH)),
        "w_pool_out": nrm((L, POOL_WIDTH, D_MODEL), POOL_WIDTH),
        "conv_w": nrm((L, SSD_CONV, SSD_CONV_DIM), SSD_CONV),
        "conv_b": small((L, SSD_CONV_DIM)),
        "dt_bias": dt_bias,
        "a_log": a_log,
        "d_skip": gain((L, SSD_HEADS)),
        "ssd_norm_g": gain((L, SSD_INNER)),
        "w_ssd_out": nrm((L, SSD_INNER, D_MODEL), SSD_INNER),
        "w_mix_out": nrm((L, D_MODEL, D_MODEL), D_MODEL),
        "mix_post_g": gain((L, D_MODEL)),
        "xattn_pre_g": gain((L, D_MODEL)),
        "mem_norm_g": gain((L, D_MODEL)),
        "w_q": nrm((L, D_MODEL, X_WIDTH), D_MODEL),
        "w_k": nrm((L, D_MODEL, X_WIDTH), D_MODEL),
        "w_v": nrm((L, D_MODEL, X_WIDTH), D_MODEL),
        "w_xo": nrm((L, X_WIDTH, D_MODEL), X_WIDTH),
        "xattn_post_g": gain((L, D_MODEL)),
        "ffn2_pre_g": gain((L, D_MODEL)),
        "ffn2_w1": nrm((L, D_MODEL, D_FF), D_MODEL),
        "ffn2_w3": nrm((L, D_MODEL, D_FF), D_MODEL),
        "ffn2_w2": nrm((L, D_FF, D_MODEL), D_FF),
        "ffn2_post_g": gain((L, D_MODEL)),
    }


def reference(x, mem, ffn1_pre_g, ffn1_w1, ffn1_w3, ffn1_w2, ffn1_post_g, mix_pre_g, w_in, b_gate,
              pool_w, pool_scale, w_pool_out, conv_w, conv_b, dt_bias, a_log, d_skip, ssd_norm_g,
              w_ssd_out, w_mix_out, mix_post_g, xattn_pre_g, mem_norm_g, w_q, w_k, w_v, w_xo,
              xattn_post_g, ffn2_pre_g, ffn2_w1, ffn2_w3, ffn2_w2, ffn2_post_g):
    h = x
    c0 = POOL_WIDTH
    c1 = c0 + SSD_INNER
    c2 = c1 + SSD_CONV_DIM
    c3 = c2 + SSD_HEADS
    for i in range(DEPTH):
        f = swiglu(rms_norm(h, ffn1_pre_g[i]), ffn1_w1[i], ffn1_w3[i], ffn1_w2[i])
        h = h + 0.5 * rms_norm(f, ffn1_post_g[i])

        u = rms_norm(h, mix_pre_g[i])
        proj = u @ w_in[i]
        p_in, z, xbc, dt_raw = proj[..., :c0], proj[..., c0:c1], proj[..., c1:c2], proj[..., c2:c3]
        gates = jax.nn.sigmoid(proj[..., c3:] + b_gate[i])
        g_a, g_b = gates[..., :D_MODEL], gates[..., D_MODEL:]
        y_a = pool_mixer(p_in, pool_w[i], pool_scale[i]) @ w_pool_out[i]
        y_b = ssd_mixer(z, xbc, dt_raw, conv_w[i], conv_b[i], dt_bias[i], a_log[i], d_skip[i],
                        ssd_norm_g[i]) @ w_ssd_out[i]
        mixed = (g_a * y_a + g_b * y_b) @ w_mix_out[i]
        h = h + rms_norm(mixed, mix_post_g[i])

        mem_n = rms_norm(mem, mem_norm_g[i])
        a = memory_cross_attention(rms_norm(h, xattn_pre_g[i]), mem_n, w_q[i], w_k[i], w_v[i], w_xo[i])
        h = h + rms_norm(a, xattn_post_g[i])

        f = swiglu(rms_norm(h, ffn2_pre_g[i]), ffn2_w1[i], ffn2_w3[i], ffn2_w2[i])
        h = h + 0.5 * rms_norm(f, ffn2_post_g[i])
    return h
```

```python
import functools
import math

import jax
import jax.numpy as jnp
import numpy as np
from jax import lax
from jax.experimental import pallas as pl
from jax.experimental.pallas import tpu as pltpu

F32 = jnp.float32
BF16 = jnp.bfloat16

D_MODEL = 2048
D_FF = 5632
CHUNK = 64
N_MEM = 256
POOL_WINDOWS = (2, 4, 8, 16)
POOL_GROUP_DIM = D_MODEL // len(POOL_WINDOWS)
POOL_HALO = 16
SSD_INNER = 4096
SSD_HEAD_DIM = 64
SSD_HEADS = SSD_INNER // SSD_HEAD_DIM
SSD_GROUPS = 8
SSD_STATE = 128
SSD_GROUP_WIDTH = SSD_INNER // SSD_GROUPS
SSD_BC = SSD_GROUPS * SSD_STATE
SSD_CONV = 4
SSD_CONV_DIM = SSD_INNER + 2 * SSD_BC
X_HEADS = 4
X_HEAD_DIM = 128
X_WIDTH = X_HEADS * X_HEAD_DIM
EPS = 1e-6

LANES = 128
MXU_DIM = 256
V7X_VMEM_LIMIT = 60 * 1024 * 1024


def _params(semantics):
    return pltpu.CompilerParams(dimension_semantics=semantics, vmem_limit_bytes=V7X_VMEM_LIMIT)


def _rms(x, g):
    return x * lax.rsqrt(jnp.mean(x * x, axis=-1, keepdims=True) + EPS) * g


def _dot(a, b):
    return jnp.dot(a, b, preferred_element_type=F32)


def _dot_nt(a, b):
    return lax.dot_general(a, b, (((1,), (1,)), ((), ())), preferred_element_type=F32)


def _const_spec(shape):
    return pl.BlockSpec(shape, lambda *_: (0,) * len(shape), pipeline_mode=pl.Buffered(1))


def _ffn_kernel(x_ref, pre_ref, w1_ref, w3_ref, w2_ref, post_ref, *rest, emit_next):
    if emit_next:
        next_ref, h_ref, un_ref, u_sc, acc_sc = rest
    else:
        h_ref, u_sc, acc_sc = rest
    j = pl.program_id(1)

    @pl.when(j == 0)
    def _():
        u_sc[...] = _rms(x_ref[...], pre_ref[...]).astype(BF16)
        acc_sc[...] = jnp.zeros_like(acc_sc)

    u = u_sc[...]
    a = _dot(u, w1_ref[...])
    b = _dot(u, w3_ref[...])
    act = (a * jax.nn.sigmoid(a) * b).astype(BF16)
    acc_sc[...] += _dot(act, w2_ref[...])

    @pl.when(j == pl.num_programs(1) - 1)
    def _():
        h = x_ref[...] + 0.5 * _rms(acc_sc[...], post_ref[...])
        h_ref[...] = h
        if emit_next:
            un_ref[...] = _rms(h, next_ref[...]).astype(BF16)


def _ffn(x, pre_g, w1, w3, w2, post_g, next_g, *, tm=512, tf=512):
    m, d = x.shape
    dff = w1.shape[1]
    emit_next = next_g is not None
    row = pl.BlockSpec((tm, d), lambda i, j: (i, 0))
    vec = pl.BlockSpec((1, d), lambda i, j: (0, 0))
    in_specs = [row, vec,
                pl.BlockSpec((d, tf), lambda i, j: (0, j)),
                pl.BlockSpec((d, tf), lambda i, j: (0, j)),
                pl.BlockSpec((tf, d), lambda i, j: (j, 0)),
                vec]
    args = [x, pre_g, w1, w3, w2, post_g]
    out_shape = [jax.ShapeDtypeStruct((m, d), F32)]
    out_specs = [row]
    if emit_next:
        in_specs.append(vec)
        args.append(next_g)
        out_shape.append(jax.ShapeDtypeStruct((m, d), BF16))
        out_specs.append(row)
    outs = pl.pallas_call(
        functools.partial(_ffn_kernel, emit_next=emit_next),
        grid=(m // tm, dff // tf),
        in_specs=in_specs,
        out_specs=out_specs,
        out_shape=out_shape,
        scratch_shapes=[pltpu.VMEM((tm, d), BF16), pltpu.VMEM((tm, d), F32)],
        compiler_params=_params(("parallel", "arbitrary")),
        name="ffn",
    )(*args)
    return outs if emit_next else (outs[0], None)


def _softplus(x):
    return jnp.maximum(x, 0.0) + jnp.log1p(jnp.exp(-jnp.abs(x)))


def _proj_kernel(u_ref, w_ref, *rest, epilogue):
    o_ref = rest[-1]
    acc = _dot(u_ref[...], w_ref[...])
    if epilogue == "silu":
        acc = acc * jax.nn.sigmoid(acc)
    elif epilogue == "sigmoid_bias":
        acc = jax.nn.sigmoid(acc + rest[0][...])
    elif epilogue == "softplus_bias":
        acc = _softplus(acc + rest[0][...])
    o_ref[...] = acc.astype(o_ref.dtype)


def _proj(u, w, bias, epilogue, *, tm=1024, tn=1024, out_dtype=F32):
    m, k = u.shape
    n = w.shape[1]
    tn = min(tn, n)
    in_specs = [pl.BlockSpec((tm, k), lambda i, j: (i, 0)),
                pl.BlockSpec((k, tn), lambda i, j: (0, j))]
    args = [u, w]
    if bias is not None:
        in_specs.append(pl.BlockSpec((1, tn), lambda i, j: (0, j)))
        args.append(bias)
    return pl.pallas_call(
        functools.partial(_proj_kernel, epilogue=epilogue),
        grid=(m // tm, n // tn),
        in_specs=in_specs,
        out_specs=pl.BlockSpec((tm, tn), lambda i, j: (i, j)),
        out_shape=jax.ShapeDtypeStruct((m, n), out_dtype),
        compiler_params=_params(("parallel", "parallel")),
        name="proj_" + epilogue,
    )(*args)


def _pool_kernel(p_ref, halo_ref, ga_ref, pw_ref, scale_ref, wo_ref, o_ref, mixed_sc, *, tiles_per_seq):
    tm = p_ref.shape[0]
    i = pl.program_id(0)
    tile_in_seq = i % tiles_per_seq
    halo = jnp.where(tile_in_seq == 0, 0.0, halo_ref[...])
    pos = tile_in_seq * tm + lax.broadcasted_iota(jnp.int32, (tm, 1), 0)
    for g, window in enumerate(POOL_WINDOWS):
        cols = slice(g * POOL_GROUP_DIM, (g + 1) * POOL_GROUP_DIM)
        x = p_ref[:, cols]
        s = jnp.concatenate([halo[:, cols], x], axis=0)
        width = 1
        while width < window:
            s = s[width:] + s[:-width]
            width *= 2
        win = s[POOL_HALO - window + 1:]
        inv_cnt = 1.0 / jnp.minimum(pos + 1, window).astype(F32)
        pooled = (win * inv_cnt - x).astype(BF16)
        mixed = _dot(pooled, pw_ref[g]) * scale_ref[:, cols]
        mixed_sc[:, cols] = mixed.astype(BF16)
    o_ref[...] = ga_ref[...] * _dot(mixed_sc[...], wo_ref[...])


def _pool(p, gates, pool_w, pool_scale, w_out, seq, *, tm=256):
    m, d = p.shape
    halo_blocks = tm // POOL_HALO
    row = pl.BlockSpec((tm, d), lambda i: (i, 0))
    return pl.pallas_call(
        functools.partial(_pool_kernel, tiles_per_seq=seq // tm),
        grid=(m // tm,),
        in_specs=[row,
                  pl.BlockSpec((POOL_HALO, d), lambda i: (jnp.maximum(i * halo_blocks - 1, 0), 0)),
                  row,
                  _const_spec(pool_w.shape),
                  _const_spec((1, d)),
                  _const_spec(w_out.shape)],
        out_specs=row,
        out_shape=jax.ShapeDtypeStruct((m, d), F32),
        scratch_shapes=[pltpu.VMEM((tm, d), BF16)],
        compiler_params=_params(("parallel",)),
        name="pool",
    )(p, p, gates, pool_w, pool_scale, w_out)


HEADS_PER_DOT = MXU_DIM // SSD_HEAD_DIM


def _ssd_constants():
    lane = np.arange(SSD_INNER)
    head_of_lane = lane // SSD_HEAD_DIM
    pos_of_lane = lane % SSD_HEAD_DIM
    expand = np.zeros((MXU_DIM, SSD_INNER), np.float32)
    for piece in range(3):
        expand[piece * SSD_HEADS + head_of_lane, lane] = 1.0
    row = np.arange(CHUNK)[:, None]
    diag = (row == pos_of_lane[None, :]).astype(np.float32)
    tril = (row >= pos_of_lane[None, :]).astype(np.float32)
    blk = np.arange(MXU_DIM) // SSD_HEAD_DIM
    blockdiag = (blk[:, None] == blk[None, :]).astype(np.float32)
    return (jnp.asarray(expand, BF16), jnp.asarray(diag), jnp.asarray(tril), jnp.asarray(blockdiag, BF16))


def _split_select(v, lower_half, pieces):
    p0 = v.astype(BF16).astype(F32)
    r1 = v - p0
    p1 = r1.astype(BF16).astype(F32)
    first = jnp.where(lower_half, p0, p1).astype(BF16)
    if pieces == 2:
        return first
    p2 = (r1 - p1).astype(BF16)
    return jnp.concatenate([first, p2], axis=1)


def _ssd_kernel(xbc_ref, dt_ref, convw_ref, convb_ref, alog_ref, dskip_ref, expand_ref, diag_ref,
                tril_ref, bd_ref, y_ref, state_sc, carry_sc):
    @pl.when(pl.program_id(1) == 0)
    def _():
        state_sc[...] = jnp.zeros_like(state_sc)
        carry_sc[...] = jnp.zeros_like(carry_sc)

    raw = xbc_ref[...]
    ext = jnp.concatenate([carry_sc[...], raw], axis=0)
    halo = carry_sc.shape[0]
    conv = convb_ref[...] + convw_ref[SSD_CONV - 1:SSD_CONV, :] * raw
    for k in range(SSD_CONV - 1):
        back = SSD_CONV - 1 - k
        conv = conv + convw_ref[k:k + 1, :] * ext[halo - back:halo - back + CHUNK]
    carry_sc[...] = raw[CHUNK - halo:]
    act = conv * jax.nn.sigmoid(conv)
    xs = act[:, :SSD_INNER]
    b_all = act[:, SSD_INNER:SSD_INNER + SSD_BC].astype(BF16)
    c_all = act[:, SSD_INNER + SSD_BC:].astype(BF16)

    dt = dt_ref[...]
    a = dt * (-jnp.exp(alog_ref[...]))
    row = lax.broadcasted_iota(jnp.int32, a.shape, 0)
    lane = lax.broadcasted_iota(jnp.int32, a.shape, 1)
    lower_half = lane < SSD_HEADS
    acs = a
    shift = 1
    while shift < CHUNK:
        acs = acs + jnp.where(row >= shift, pltpu.roll(acs, shift, axis=0), 0.0)
        shift *= 2

    colb = _dot(_split_select(acs, lower_half, 3), expand_ref[...])
    dt_x = _dot(_split_select(dt, lower_half, 2), expand_ref[0:LANES, :])
    rowb = jnp.sum(colb * diag_ref[...], axis=0, keepdims=True)
    decay_ls = jnp.exp(jnp.where(tril_ref[...] > 0.0, colb - rowb, -jnp.inf))
    total = colb[CHUNK - 1:CHUNK, :]
    from_start = jnp.exp(colb)
    to_end = jnp.exp(total - colb)
    chunk_decay = jnp.exp(total)

    xdt = xs * dt_x
    xdt_b = xdt.astype(BF16)
    xend_b = (xdt * to_end).astype(BF16)
    skip = dskip_ref[...] * xs

    for g in range(SSD_GROUPS):
        bg = b_all[:, g * SSD_STATE:(g + 1) * SSD_STATE]
        cg = c_all[:, g * SSD_STATE:(g + 1) * SSD_STATE]
        cols = slice(g * SSD_GROUP_WIDTH, (g + 1) * SSD_GROUP_WIDTH)
        cb2 = _dot_nt(cg, jnp.concatenate([bg, bg], axis=0))
        cb = jnp.concatenate([cb2] * (SSD_GROUP_WIDTH // LANES), axis=1)
        gmat = (cb * decay_ls[:, cols]).astype(BF16)
        state = state_sc[g]
        y_g = _dot(cg, state.astype(BF16)) * from_start[:, cols] + skip[:, cols]
        diag_parts = []
        for part in range(SSD_GROUP_WIDTH // MXU_DIM):
            lo = g * SSD_GROUP_WIDTH + part * MXU_DIM
            x_part = xdt_b[:, lo:lo + MXU_DIM]
            x_bd = jnp.concatenate([x_part] * HEADS_PER_DOT, axis=0) * bd_ref[...]
            diag_parts.append(_dot(gmat[:, part * MXU_DIM:(part + 1) * MXU_DIM], x_bd))
        y_ref[:, cols] = y_g + jnp.concatenate(diag_parts, axis=1)
        new_state = lax.dot_general(bg, xend_b[:, cols], (((0,), (0,)), ((), ())),
                                    preferred_element_type=F32)
        state_sc[g] = state * chunk_decay[:, cols] + new_state


def _ssd(xbc, dt2, conv_w, conv_b, a_log2, dskip_x, batch):
    m = xbc.shape[0]
    chunks = m // batch // CHUNK
    expand, diag, tril, blockdiag = _ssd_constants()
    tok = lambda width: pl.BlockSpec((CHUNK, width), lambda b, c: (b * chunks + c, 0))
    return pl.pallas_call(
        _ssd_kernel,
        grid=(batch, chunks),
        in_specs=[tok(SSD_CONV_DIM), tok(LANES),
                  _const_spec(conv_w.shape), _const_spec(conv_b.shape), _const_spec(a_log2.shape),
                  _const_spec(dskip_x.shape), _const_spec(expand.shape), _const_spec(diag.shape),
                  _const_spec(tril.shape), _const_spec(blockdiag.shape)],
        out_specs=tok(SSD_INNER),
        out_shape=jax.ShapeDtypeStruct((m, SSD_INNER), F32),
        scratch_shapes=[pltpu.VMEM((SSD_GROUPS, SSD_STATE, SSD_GROUP_WIDTH), F32),
                        pltpu.VMEM((8, SSD_CONV_DIM), F32)],
        compiler_params=_params(("arbitrary", "arbitrary")),
        name="ssd",
    )(xbc, dt2, conv_w, conv_b, a_log2, dskip_x, expand, diag, tril, blockdiag)


def _merge_kernel(y_ref, z_ref, ng_ref, wso_ref, gaya_ref, gb_ref, wmix_ref, post_ref, h_ref, next_ref,
                  ho_ref, un_ref, yn_sc):
    yz = y_ref[...] * z_ref[...]
    for g in range(SSD_GROUPS):
        cols = slice(g * SSD_GROUP_WIDTH, (g + 1) * SSD_GROUP_WIDTH)
        yn_sc[:, cols] = _rms(yz[:, cols], ng_ref[:, cols]).astype(BF16)
    y_b = _dot(yn_sc[...], wso_ref[...])
    merged = (gaya_ref[...] + gb_ref[...] * y_b).astype(BF16)
    h = h_ref[...] + _rms(_dot(merged, wmix_ref[...]), post_ref[...])
    ho_ref[...] = h
    un_ref[...] = _rms(h, next_ref[...]).astype(BF16)


def _merge(y, z, norm_g, w_ssd_out, gaya, gates, w_mix, post_g, h, next_g, *, tm=128):
    m, d = h.shape
    wide = pl.BlockSpec((tm, SSD_INNER), lambda i: (i, 0))
    row = pl.BlockSpec((tm, d), lambda i: (i, 0))
    return pl.pallas_call(
        _merge_kernel,
        grid=(m // tm,),
        in_specs=[wide, wide, _const_spec((1, SSD_INNER)), _const_spec(w_ssd_out.shape),
                  row,
                  pl.BlockSpec((tm, d), lambda i: (i, 1)),
                  _const_spec(w_mix.shape), _const_spec((1, d)), row, _const_spec((1, d))],
        out_specs=[row, row],
        out_shape=[jax.ShapeDtypeStruct((m, d), F32), jax.ShapeDtypeStruct((m, d), BF16)],
        scratch_shapes=[pltpu.VMEM((tm, SSD_INNER), BF16)],
        compiler_params=_params(("parallel",)),
        name="merge",
    )(y, z, norm_g, w_ssd_out, gaya, gates, w_mix, post_g, h, next_g)


def _kv_kernel(mem_ref, g_ref, wk_ref, wv_ref, k_ref, v_ref):
    mem_n = _rms(mem_ref[...], g_ref[...]).astype(BF16)
    k_ref[...] = _dot(mem_n, wk_ref[...]).astype(BF16)
    v_ref[...] = _dot(mem_n, wv_ref[...]).astype(BF16)


def _kv(mem, g, w_k, w_v):
    rows = mem.shape[0]
    out = jax.ShapeDtypeStruct((rows, X_WIDTH), BF16)
    return pl.pallas_call(_kv_kernel, out_shape=[out, out], name="kv",
                          compiler_params=pltpu.CompilerParams(vmem_limit_bytes=V7X_VMEM_LIMIT))(mem, g, w_k, w_v)


def _attn_kernel(u_ref, h_ref, wq_ref, k_ref, v_ref, wo_ref, post_ref, ho_ref, o_sc):
    q = _dot(u_ref[...], wq_ref[...]).astype(BF16)
    for hd in range(X_HEADS):
        cols = slice(hd * X_HEAD_DIM, (hd + 1) * X_HEAD_DIM)
        s = _dot_nt(q[:, cols], k_ref[:, cols]) * (1.0 / math.sqrt(X_HEAD_DIM))
        e = jnp.exp(s - jnp.max(s, axis=-1, keepdims=True))
        p = (e / jnp.sum(e, axis=-1, keepdims=True)).astype(BF16)
        o_sc[:, cols] = _dot(p, v_ref[:, cols]).astype(BF16)
    ho_ref[...] = h_ref[...] + _rms(_dot(o_sc[...], wo_ref[...]), post_ref[...])


def _attn(u, h, w_q, k, v, w_o, post_g, seq, *, tm=512):
    m, d = h.shape
    tiles_per_seq = seq // tm
    row = pl.BlockSpec((tm, d), lambda i: (i, 0))
    mem = pl.BlockSpec((N_MEM, X_WIDTH), lambda i: (i // tiles_per_seq, 0))
    return pl.pallas_call(
        _attn_kernel,
        grid=(m // tm,),
        in_specs=[row, row, _const_spec(w_q.shape), mem, mem, _const_spec(w_o.shape),
                  _const_spec((1, d))],
        out_specs=row,
        out_shape=jax.ShapeDtypeStruct((m, d), F32),
        scratch_shapes=[pltpu.VMEM((tm, X_WIDTH), BF16)],
        compiler_params=_params(("parallel",)),
        name="attn",
    )(u, h, w_q, k, v, w_o, post_g)


def kernel(x, mem, ffn1_pre_g, ffn1_w1, ffn1_w3, ffn1_w2, ffn1_post_g, mix_pre_g, w_in, b_gate, pool_w, pool_scale, w_pool_out, conv_w, conv_b, dt_bias, a_log, d_skip, ssd_norm_g, w_ssd_out, w_mix_out, mix_post_g, xattn_pre_g, mem_norm_g, w_q, w_k, w_v, w_xo, xattn_post_g, ffn2_pre_g, ffn2_w1, ffn2_w3, ffn2_w2, ffn2_post_g):
    batch, seq, d = x.shape
    depth = ffn1_w1.shape[0]
    c0 = D_MODEL
    c1 = c0 + SSD_INNER
    c2 = c1 + SSD_CONV_DIM
    c3 = c2 + SSD_HEADS
    vec = lambda g: g.reshape(1, -1).astype(F32)
    twice = lambda t: jnp.concatenate([t, t], axis=-1)

    h = x.reshape(batch * seq, d)
    mem2 = mem.reshape(batch * N_MEM, d)
    u = None
    for i in range(depth):
        bf = lambda w: w[i].astype(BF16)
        w_in_i = w_in[i]

        h, u = _ffn(h, vec(ffn1_pre_g[i]), bf(ffn1_w1), bf(ffn1_w3), bf(ffn1_w2), vec(ffn1_post_g[i]),
                    vec(mix_pre_g[i]))

        p_in = _proj(u, w_in_i[:, :c0].astype(BF16), None, "none")
        z_act = _proj(u, w_in_i[:, c0:c1].astype(BF16), None, "silu")
        xbc = _proj(u, w_in_i[:, c1:c2].astype(BF16), None, "none")
        dt2 = _proj(u, twice(w_in_i[:, c2:c3]).astype(BF16), twice(vec(dt_bias[i])), "softplus_bias")
        gates = _proj(u, w_in_i[:, c3:].astype(BF16), vec(b_gate[i]), "sigmoid_bias")

        gaya = _pool(p_in, gates, bf(pool_w), vec(pool_scale[i]), bf(w_pool_out), seq)
        y = _ssd(xbc, dt2, conv_w[i], vec(conv_b[i]), twice(vec(a_log[i])),
                 jnp.repeat(vec(d_skip[i]), SSD_HEAD_DIM, axis=1), batch)
        h, u = _merge(y, z_act, vec(ssd_norm_g[i]), bf(w_ssd_out), gaya, gates, bf(w_mix_out),
                      vec(mix_post_g[i]), h, vec(xattn_pre_g[i]))

        k, v = _kv(mem2, vec(mem_norm_g[i]), bf(w_k), bf(w_v))
        h = _attn(u, h, bf(w_q), k, v, bf(w_xo), vec(xattn_post_g[i]), seq)

        last = i == depth - 1
        h, u = _ffn(h, vec(ffn2_pre_g[i]), bf(ffn2_w1), bf(ffn2_w3), bf(ffn2_w2), vec(ffn2_post_g[i]),
                    None if last else vec(ffn1_pre_g[i + 1]))
    return h.reshape(batch, seq, d)
```

```python
import functools
import math

import jax
import jax.numpy as jnp
import numpy as np
from jax import lax
from jax.experimental import pallas as pl
from jax.experimental.pallas import tpu as pltpu

F32 = jnp.float32
BF16 = jnp.bfloat16

D_MODEL = 2048
D_FF = 5632
CHUNK = 64
N_MEM = 256
POOL_WINDOWS = (2, 4, 8, 16)
POOL_GROUP_DIM = D_MODEL // len(POOL_WINDOWS)
POOL_HALO = 16
SSD_INNER = 4096
SSD_HEAD_DIM = 64
SSD_HEADS = SSD_INNER // SSD_HEAD_DIM
SSD_GROUPS = 8
SSD_STATE = 128
SSD_GROUP_WIDTH = SSD_INNER // SSD_GROUPS
SSD_BC = SSD_GROUPS * SSD_STATE
SSD_CONV = 4
SSD_CONV_DIM = SSD_INNER + 2 * SSD_BC
X_HEADS = 4
X_HEAD_DIM = 128
X_WIDTH = X_HEADS * X_HEAD_DIM
EPS = 1e-6

LANES = 128
MXU_DIM = 256
V7X_VMEM_LIMIT = 60 * 1024 * 1024


def _params(semantics):
    return pltpu.CompilerParams(dimension_semantics=semantics, vmem_limit_bytes=V7X_VMEM_LIMIT)


def _rms(x, g):
    return x * lax.rsqrt(jnp.mean(x * x, axis=-1, keepdims=True) + EPS) * g


def _dot(a, b):
    return jnp.dot(a, b, preferred_element_type=F32)


def _dot_nt(a, b):
    return lax.dot_general(a, b, (((1,), (1,)), ((), ())), preferred_element_type=F32)


def _const_spec(shape):
    return pl.BlockSpec(shape, lambda *_: (0,) * len(shape), pipeline_mode=pl.Buffered(1))


def _ffn_kernel(x_ref, pre_ref, w1_ref, w3_ref, w2_ref, post_ref, *rest, emit_next):
    if emit_next:
        next_ref, h_ref, un_ref, u_sc, acc_sc = rest
    else:
        h_ref, u_sc, acc_sc = rest
    j = pl.program_id(1)

    @pl.when(j == 0)
    def _():
        u_sc[...] = _rms(x_ref[...], pre_ref[...]).astype(BF16)
        acc_sc[...] = jnp.zeros_like(acc_sc)

    u = u_sc[...]
    a = _dot(u, w1_ref[...])
    b = _dot(u, w3_ref[...])
    act = (a * jax.nn.sigmoid(a) * b).astype(BF16)
    acc_sc[...] += _dot(act, w2_ref[...])

    @pl.when(j == pl.num_programs(1) - 1)
    def _():
        h = x_ref[...] + 0.5 * _rms(acc_sc[...], post_ref[...])
        h_ref[...] = h
        if emit_next:
            un_ref[...] = _rms(h, next_ref[...]).astype(BF16)


def _ffn(x, pre_g, w1, w3, w2, post_g, next_g, *, tm=512, tf=512):
    m, d = x.shape
    dff = w1.shape[1]
    emit_next = next_g is not None
    row = pl.BlockSpec((tm, d), lambda i, j: (i, 0))
    vec = pl.BlockSpec((1, d), lambda i, j: (0, 0))
    in_specs = [row, vec,
                pl.BlockSpec((d, tf), lambda i, j: (0, j)),
                pl.BlockSpec((d, tf), lambda i, j: (0, j)),
                pl.BlockSpec((tf, d), lambda i, j: (j, 0)),
                vec]
    args = [x, pre_g, w1, w3, w2, post_g]
    out_shape = [jax.ShapeDtypeStruct((m, d), F32)]
    out_specs = [row]
    if emit_next:
        in_specs.append(vec)
        args.append(next_g)
        out_shape.append(jax.ShapeDtypeStruct((m, d), BF16))
        out_specs.append(row)
    outs = pl.pallas_call(
        functools.partial(_ffn_kernel, emit_next=emit_next),
        grid=(m // tm, dff // tf),
        in_specs=in_specs,
        out_specs=out_specs,
        out_shape=out_shape,
        scratch_shapes=[pltpu.VMEM((tm, d), BF16), pltpu.VMEM((tm, d), F32)],
        compiler_params=_params(("parallel", "arbitrary")),
        name="ffn",
    )(*args)
    return outs if emit_next else (outs[0], None)


def _softplus(x):
    return jnp.maximum(x, 0.0) + jnp.log1p(jnp.exp(-jnp.abs(x)))


CONV_CARRY = 8
PROJ_EXTRA_INPUTS = {"none": 0, "silu": 0, "sigmoid_bias": 1, "softplus_bias": 1, "conv_silu": 2}


def _proj_kernel(u_ref, w_ref, *rest, epilogue, cast_weight, tiles_per_seq):
    rest = list(rest)
    extra = [rest.pop(0) for _ in range(PROJ_EXTRA_INPUTS[epilogue])]
    o_ref = rest.pop(0)
    i = pl.program_id(1)
    if cast_weight:
        w_sc = rest.pop(0)

        @pl.when(i == 0)
        def _():
            w_sc[...] = w_ref[...].astype(BF16)

        w = w_sc[...]
    else:
        w = w_ref[...]
    acc = _dot(u_ref[...], w)
    if epilogue == "silu":
        acc = acc * jax.nn.sigmoid(acc)
    elif epilogue == "sigmoid_bias":
        acc = jax.nn.sigmoid(acc + extra[0][...])
    elif epilogue == "softplus_bias":
        acc = _softplus(acc + extra[0][...])
    elif epilogue == "conv_silu":
        convw_ref, convb_ref = extra
        carry_sc = rest.pop(0)
        tm = acc.shape[0]
        carry = jnp.where(i % tiles_per_seq == 0, 0.0, carry_sc[...])
        carry_sc[...] = acc[tm - CONV_CARRY:]
        ext = jnp.concatenate([carry, acc], axis=0)
        conv = convb_ref[...] + convw_ref[SSD_CONV - 1:SSD_CONV, :] * acc
        for k in range(SSD_CONV - 1):
            start = CONV_CARRY - (SSD_CONV - 1 - k)
            conv = conv + convw_ref[k:k + 1, :] * ext[start:start + tm]
        acc = conv * jax.nn.sigmoid(conv)
    o_ref[...] = acc.astype(o_ref.dtype)


def _proj(u, w, col_start, n, epilogue, extra=(), *, seq, layer=None, tm=1024, tn=1024, out_dtype=F32):
    m, k = u.shape
    tn = min(tn, n)
    assert col_start % tn == 0 and n % tn == 0 and m % tm == 0 and seq % tm == 0
    cast_weight = w.dtype != BF16
    col_block = col_start // tn
    if layer is None:
        w_spec = pl.BlockSpec((k, tn), lambda j, i: (0, col_block + j))
    else:
        w_spec = pl.BlockSpec((None, k, tn), lambda j, i: (layer, 0, col_block + j))
    in_specs = [pl.BlockSpec((tm, k), lambda j, i: (i, 0)), w_spec]
    for e in extra:
        in_specs.append(pl.BlockSpec((e.shape[0], tn), lambda j, i: (0, j)))
    scratch = []
    if cast_weight:
        scratch.append(pltpu.VMEM((k, tn), BF16))
    if epilogue == "conv_silu":
        scratch.append(pltpu.VMEM((CONV_CARRY, tn), F32))
    return pl.pallas_call(
        functools.partial(_proj_kernel, epilogue=epilogue, cast_weight=cast_weight, tiles_per_seq=seq // tm),
        grid=(n // tn, m // tm),
        in_specs=in_specs,
        out_specs=pl.BlockSpec((tm, tn), lambda j, i: (i, j)),
        out_shape=jax.ShapeDtypeStruct((m, n), out_dtype),
        scratch_shapes=scratch,
        compiler_params=_params(("parallel", "arbitrary")),
        name="proj_" + epilogue,
    )(u, w, *extra)


def _pool_kernel(p_ref, halo_ref, ga_ref, pw_ref, scale_ref, wo_ref, o_ref, mixed_sc, *, tiles_per_seq):
    tm = p_ref.shape[0]
    i = pl.program_id(0)
    tile_in_seq = i % tiles_per_seq
    halo = jnp.where(tile_in_seq == 0, 0.0, halo_ref[...])
    pos = tile_in_seq * tm + lax.broadcasted_iota(jnp.int32, (tm, 1), 0)
    for g, window in enumerate(POOL_WINDOWS):
        cols = slice(g * POOL_GROUP_DIM, (g + 1) * POOL_GROUP_DIM)
        x = p_ref[:, cols]
        s = jnp.concatenate([halo[:, cols], x], axis=0)
        width = 1
        while width < window:
            s = s[width:] + s[:-width]
            width *= 2
        win = s[POOL_HALO - window + 1:]
        inv_cnt = 1.0 / jnp.minimum(pos + 1, window).astype(F32)
        pooled = (win * inv_cnt - x).astype(BF16)
        mixed = _dot(pooled, pw_ref[g]) * scale_ref[:, cols]
        mixed_sc[:, cols] = mixed.astype(BF16)
    o_ref[...] = ga_ref[...] * _dot(mixed_sc[...], wo_ref[...])


def _pool(p, gates, pool_w, pool_scale, w_out, seq, *, tm=256):
    m, d = p.shape
    halo_blocks = tm // POOL_HALO
    row = pl.BlockSpec((tm, d), lambda i: (i, 0))
    return pl.pallas_call(
        functools.partial(_pool_kernel, tiles_per_seq=seq // tm),
        grid=(m // tm,),
        in_specs=[row,
                  pl.BlockSpec((POOL_HALO, d), lambda i: (jnp.maximum(i * halo_blocks - 1, 0), 0)),
                  row,
                  _const_spec(pool_w.shape),
                  _const_spec((1, d)),
                  _const_spec(w_out.shape)],
        out_specs=row,
        out_shape=jax.ShapeDtypeStruct((m, d), F32),
        scratch_shapes=[pltpu.VMEM((tm, d), BF16)],
        compiler_params=_params(("parallel",)),
        name="pool",
    )(p, p, gates, pool_w, pool_scale, w_out)


HEADS_PER_DOT = MXU_DIM // SSD_HEAD_DIM


def _ssd_constants():
    lane = np.arange(SSD_INNER)
    head_of_lane = lane // SSD_HEAD_DIM
    pos_of_lane = lane % SSD_HEAD_DIM
    expand = np.zeros((MXU_DIM, SSD_INNER), np.float32)
    for piece in range(3):
        expand[piece * SSD_HEADS + head_of_lane, lane] = 1.0
    row = np.arange(CHUNK)[:, None]
    diag = (row == pos_of_lane[None, :]).astype(np.float32)
    tril = (row >= pos_of_lane[None, :]).astype(np.float32)
    blk = np.arange(MXU_DIM) // SSD_HEAD_DIM
    blockdiag = (blk[:, None] == blk[None, :]).astype(np.float32)
    return (jnp.asarray(expand, BF16), jnp.asarray(diag), jnp.asarray(tril), jnp.asarray(blockdiag, BF16))


def _split_select(v, lower_half, pieces):
    p0 = v.astype(BF16).astype(F32)
    r1 = v - p0
    p1 = r1.astype(BF16).astype(F32)
    first = jnp.where(lower_half, p0, p1).astype(BF16)
    if pieces == 2:
        return first
    p2 = (r1 - p1).astype(BF16)
    return jnp.concatenate([first, p2], axis=1)


def _ssd_kernel(xbc_ref, dt_ref, alog_ref, dskip_ref, expand_ref, diag_ref, tril_ref, bd_ref, y_ref,
                state_sc):
    @pl.when(pl.program_id(1) == 0)
    def _():
        state_sc[...] = jnp.zeros_like(state_sc)

    xs = xbc_ref[:, :SSD_INNER]
    b_all = xbc_ref[:, SSD_INNER:SSD_INNER + SSD_BC].astype(BF16)
    c_all = xbc_ref[:, SSD_INNER + SSD_BC:].astype(BF16)

    dt = dt_ref[...]
    a = dt * (-jnp.exp(alog_ref[...]))
    row = lax.broadcasted_iota(jnp.int32, a.shape, 0)
    lane = lax.broadcasted_iota(jnp.int32, a.shape, 1)
    lower_half = lane < SSD_HEADS
    acs = a
    shift = 1
    while shift < CHUNK:
        acs = acs + jnp.where(row >= shift, pltpu.roll(acs, shift, axis=0), 0.0)
        shift *= 2

    colb = _dot(_split_select(acs, lower_half, 3), expand_ref[...])
    dt_x = _dot(_split_select(dt, lower_half, 2), expand_ref[0:LANES, :])
    rowb = jnp.sum(colb * diag_ref[...], axis=0, keepdims=True)
    decay_ls = jnp.exp(jnp.where(tril_ref[...] > 0.0, colb - rowb, -jnp.inf))
    total = colb[CHUNK - 1:CHUNK, :]
    from_start = jnp.exp(colb)
    to_end = jnp.exp(total - colb)
    chunk_decay = jnp.exp(total)

    xdt = xs * dt_x
    xdt_b = xdt.astype(BF16)
    xend_b = (xdt * to_end).astype(BF16)
    skip = dskip_ref[...] * xs

    for g in range(SSD_GROUPS):
        bg = b_all[:, g * SSD_STATE:(g + 1) * SSD_STATE]
        cg = c_all[:, g * SSD_STATE:(g + 1) * SSD_STATE]
        cols = slice(g * SSD_GROUP_WIDTH, (g + 1) * SSD_GROUP_WIDTH)
        cb2 = _dot_nt(cg, jnp.concatenate([bg, bg], axis=0))
        cb = jnp.concatenate([cb2] * (SSD_GROUP_WIDTH // LANES), axis=1)
        gmat = (cb * decay_ls[:, cols]).astype(BF16)
        state = state_sc[g]
        y_g = _dot(cg, state.astype(BF16)) * from_start[:, cols] + skip[:, cols]
        diag_parts = []
        for part in range(SSD_GROUP_WIDTH // MXU_DIM):
            lo = g * SSD_GROUP_WIDTH + part * MXU_DIM
            x_part = xdt_b[:, lo:lo + MXU_DIM]
            x_bd = jnp.concatenate([x_part] * HEADS_PER_DOT, axis=0) * bd_ref[...]
            diag_parts.append(_dot(gmat[:, part * MXU_DIM:(part + 1) * MXU_DIM], x_bd))
        y_ref[:, cols] = y_g + jnp.concatenate(diag_parts, axis=1)
        new_state = lax.dot_general(bg, xend_b[:, cols], (((0,), (0,)), ((), ())),
                                    preferred_element_type=F32)
        state_sc[g] = state * chunk_decay[:, cols] + new_state


def _ssd(xbc, dt2, a_log2, dskip_x, batch):
    m = xbc.shape[0]
    chunks = m // batch // CHUNK
    expand, diag, tril, blockdiag = _ssd_constants()
    tok = lambda width: pl.BlockSpec((CHUNK, width), lambda b, c: (b * chunks + c, 0))
    return pl.pallas_call(
        _ssd_kernel,
        grid=(batch, chunks),
        in_specs=[tok(SSD_CONV_DIM), tok(LANES), _const_spec(a_log2.shape),
                  _const_spec(dskip_x.shape), _const_spec(expand.shape), _const_spec(diag.shape),
                  _const_spec(tril.shape), _const_spec(blockdiag.shape)],
        out_specs=tok(SSD_INNER),
        out_shape=jax.ShapeDtypeStruct((m, SSD_INNER), F32),
        scratch_shapes=[pltpu.VMEM((SSD_GROUPS, SSD_STATE, SSD_GROUP_WIDTH), F32)],
        compiler_params=_params(("arbitrary", "arbitrary")),
        name="ssd",
    )(xbc, dt2, a_log2, dskip_x, expand, diag, tril, blockdiag)


def _merge_kernel(y_ref, z_ref, ng_ref, wso_ref, gaya_ref, gb_ref, wmix_ref, post_ref, h_ref, next_ref,
                  ho_ref, un_ref, yn_sc):
    yz = y_ref[...] * z_ref[...]
    for g in range(SSD_GROUPS):
        cols = slice(g * SSD_GROUP_WIDTH, (g + 1) * SSD_GROUP_WIDTH)
        yn_sc[:, cols] = _rms(yz[:, cols], ng_ref[:, cols]).astype(BF16)
    y_b = _dot(yn_sc[...], wso_ref[...])
    merged = (gaya_ref[...] + gb_ref[...] * y_b).astype(BF16)
    h = h_ref[...] + _rms(_dot(merged, wmix_ref[...]), post_ref[...])
    ho_ref[...] = h
    un_ref[...] = _rms(h, next_ref[...]).astype(BF16)


def _merge(y, z, norm_g, w_ssd_out, gaya, gates, w_mix, post_g, h, next_g, *, tm=128):
    m, d = h.shape
    wide = pl.BlockSpec((tm, SSD_INNER), lambda i: (i, 0))
    row = pl.BlockSpec((tm, d), lambda i: (i, 0))
    return pl.pallas_call(
        _merge_kernel,
        grid=(m // tm,),
        in_specs=[wide, wide, _const_spec((1, SSD_INNER)), _const_spec(w_ssd_out.shape),
                  row,
                  pl.BlockSpec((tm, d), lambda i: (i, 1)),
                  _const_spec(w_mix.shape), _const_spec((1, d)), row, _const_spec((1, d))],
        out_specs=[row, row],
        out_shape=[jax.ShapeDtypeStruct((m, d), F32), jax.ShapeDtypeStruct((m, d), BF16)],
        scratch_shapes=[pltpu.VMEM((tm, SSD_INNER), BF16)],
        compiler_params=_params(("parallel",)),
        name="merge",
    )(y, z, norm_g, w_ssd_out, gaya, gates, w_mix, post_g, h, next_g)


def _kv_kernel(mem_ref, g_ref, wk_ref, wv_ref, k_ref, v_ref):
    mem_n = _rms(mem_ref[...], g_ref[...]).astype(BF16)
    k_ref[...] = _dot(mem_n, wk_ref[...]).astype(BF16)
    v_ref[...] = _dot(mem_n, wv_ref[...]).astype(BF16)


def _kv(mem, g, w_k, w_v):
    rows = mem.shape[0]
    out = jax.ShapeDtypeStruct((rows, X_WIDTH), BF16)
    return pl.pallas_call(_kv_kernel, out_shape=[out, out], name="kv",
                          compiler_params=pltpu.CompilerParams(vmem_limit_bytes=V7X_VMEM_LIMIT))(mem, g, w_k, w_v)


def _attn_kernel(u_ref, h_ref, wq_ref, k_ref, v_ref, wo_ref, post_ref, ho_ref, o_sc):
    q = _dot(u_ref[...], wq_ref[...]).astype(BF16)
    for hd in range(X_HEADS):
        cols = slice(hd * X_HEAD_DIM, (hd + 1) * X_HEAD_DIM)
        s = _dot_nt(q[:, cols], k_ref[:, cols]) * (1.0 / math.sqrt(X_HEAD_DIM))
        e = jnp.exp(s - jnp.max(s, axis=-1, keepdims=True))
        p = (e / jnp.sum(e, axis=-1, keepdims=True)).astype(BF16)
        o_sc[:, cols] = _dot(p, v_ref[:, cols]).astype(BF16)
    ho_ref[...] = h_ref[...] + _rms(_dot(o_sc[...], wo_ref[...]), post_ref[...])


def _attn(u, h, w_q, k, v, w_o, post_g, seq, *, tm=512):
    m, d = h.shape
    tiles_per_seq = seq // tm
    row = pl.BlockSpec((tm, d), lambda i: (i, 0))
    mem = pl.BlockSpec((N_MEM, X_WIDTH), lambda i: (i // tiles_per_seq, 0))
    return pl.pallas_call(
        _attn_kernel,
        grid=(m // tm,),
        in_specs=[row, row, _const_spec(w_q.shape), mem, mem, _const_spec(w_o.shape),
                  _const_spec((1, d))],
        out_specs=row,
        out_shape=jax.ShapeDtypeStruct((m, d), F32),
        scratch_shapes=[pltpu.VMEM((tm, X_WIDTH), BF16)],
        compiler_params=_params(("parallel",)),
        name="attn",
    )(u, h, w_q, k, v, w_o, post_g)


def kernel(x, mem, ffn1_pre_g, ffn1_w1, ffn1_w3, ffn1_w2, ffn1_post_g, mix_pre_g, w_in, b_gate, pool_w, pool_scale, w_pool_out, conv_w, conv_b, dt_bias, a_log, d_skip, ssd_norm_g, w_ssd_out, w_mix_out, mix_post_g, xattn_pre_g, mem_norm_g, w_q, w_k, w_v, w_xo, xattn_post_g, ffn2_pre_g, ffn2_w1, ffn2_w3, ffn2_w2, ffn2_post_g):
    batch, seq, d = x.shape
    depth = ffn1_w1.shape[0]
    c0 = D_MODEL
    c1 = c0 + SSD_INNER
    c2 = c1 + SSD_CONV_DIM
    c3 = c2 + SSD_HEADS
    vec = lambda g: g.reshape(1, -1).astype(F32)
    twice = lambda t: jnp.concatenate([t, t], axis=-1)

    h = x.reshape(batch * seq, d)
    mem2 = mem.reshape(batch * N_MEM, d)
    u = None
    for i in range(depth):
        bf = lambda w: w[i].astype(BF16)

        h, u = _ffn(h, vec(ffn1_pre_g[i]), bf(ffn1_w1), bf(ffn1_w3), bf(ffn1_w2), vec(ffn1_post_g[i]),
                    vec(mix_pre_g[i]))

        p_in = _proj(u, w_in, 0, c0, "none", seq=seq, layer=i)
        z_act = _proj(u, w_in, c0, c1 - c0, "silu", seq=seq, layer=i)
        xbc = _proj(u, w_in, c1, c2 - c1, "conv_silu", (conv_w[i], vec(conv_b[i])), seq=seq, layer=i, tn=512)
        dt2 = _proj(u, twice(w_in[i, :, c2:c3]).astype(BF16), 0, 2 * SSD_HEADS, "softplus_bias",
                    (twice(vec(dt_bias[i])),), seq=seq)
        gates = _proj(u, w_in[i, :, c3:].astype(BF16), 0, 2 * D_MODEL, "sigmoid_bias", (vec(b_gate[i]),),
                      seq=seq)

        gaya = _pool(p_in, gates, bf(pool_w), vec(pool_scale[i]), bf(w_pool_out), seq)
        y = _ssd(xbc, dt2, twice(vec(a_log[i])), jnp.repeat(vec(d_skip[i]), SSD_HEAD_DIM, axis=1), batch)
        h, u = _merge(y, z_act, vec(ssd_norm_g[i]), bf(w_ssd_out), gaya, gates, bf(w_mix_out),
                      vec(mix_post_g[i]), h, vec(xattn_pre_g[i]))

        k, v = _kv(mem2, vec(mem_norm_g[i]), bf(w_k), bf(w_v))
        h = _attn(u, h, bf(w_q), k, v, bf(w_xo), vec(xattn_post_g[i]), seq)

        last = i == depth - 1
        h, u = _ffn(h, vec(ffn2_pre_g[i]), bf(ffn2_w1), bf(ffn2_w3), bf(ffn2_w2), vec(ffn2_post_g[i]),
                    None if last else vec(ffn1_pre_g[i + 1]))
    return h.reshape(batch, seq, d)
```

```python
import functools
import math

import jax
import jax.numpy as jnp
import numpy as np
from jax import lax
from jax.experimental import pallas as pl
from jax.experimental.pallas import tpu as pltpu

F32 = jnp.float32
BF16 = jnp.bfloat16

D_MODEL = 2048
D_FF = 5632
CHUNK = 64
N_MEM = 256
POOL_WINDOWS = (2, 4, 8, 16)
POOL_GROUP_DIM = D_MODEL // len(POOL_WINDOWS)
POOL_HALO = 16
SSD_INNER = 4096
SSD_HEAD_DIM = 64
SSD_HEADS = SSD_INNER // SSD_HEAD_DIM
SSD_GROUPS = 8
SSD_STATE = 128
SSD_GROUP_WIDTH = SSD_INNER // SSD_GROUPS
SSD_BC = SSD_GROUPS * SSD_STATE
SSD_CONV = 4
SSD_CONV_DIM = SSD_INNER + 2 * SSD_BC
X_HEADS = 4
X_HEAD_DIM = 128
X_WIDTH = X_HEADS * X_HEAD_DIM
EPS = 1e-6
LOG2_E = math.log2(math.e)

LANES = 128
MXU_DIM = 256
V7X_VMEM_LIMIT = 60 * 1024 * 1024


def _params(semantics, flags=None):
    return pltpu.CompilerParams(dimension_semantics=semantics, vmem_limit_bytes=V7X_VMEM_LIMIT, flags=flags)


def _rms(x, g):
    return x * lax.rsqrt(jnp.mean(x * x, axis=-1, keepdims=True) + EPS) * g


def _dot(a, b):
    return jnp.dot(a, b, preferred_element_type=F32)


def _dot_nt(a, b):
    return lax.dot_general(a, b, (((1,), (1,)), ((), ())), preferred_element_type=F32)


def _const_spec(shape):
    return pl.BlockSpec(shape, lambda *_: (0,) * len(shape), pipeline_mode=pl.Buffered(1))


def _ffn_kernel(x_ref, pre_ref, w1_ref, w3_ref, w2_ref, post_ref, *rest, emit_next):
    if emit_next:
        next_ref, h_ref, un_ref, u_sc, acc_sc = rest
    else:
        h_ref, u_sc, acc_sc = rest
    j = pl.program_id(1)

    @pl.when(j == 0)
    def _():
        u_sc[...] = _rms(x_ref[...], pre_ref[...]).astype(BF16)
        acc_sc[...] = jnp.zeros_like(acc_sc)

    u = u_sc[...]
    a = _dot(u, w1_ref[...])
    b = _dot(u, w3_ref[...])
    act = (a * jax.nn.sigmoid(a) * b).astype(BF16)
    acc_sc[...] += _dot(act, w2_ref[...])

    @pl.when(j == pl.num_programs(1) - 1)
    def _():
        h = x_ref[...] + 0.5 * _rms(acc_sc[...], post_ref[...])
        h_ref[...] = h
        if emit_next:
            un_ref[...] = _rms(h, next_ref[...]).astype(BF16)


def _ffn(x, pre_g, w1, w3, w2, post_g, next_g, *, tm=512, tf=512):
    m, d = x.shape
    dff = w1.shape[1]
    emit_next = next_g is not None
    row = pl.BlockSpec((tm, d), lambda i, j: (i, 0))
    vec = pl.BlockSpec((1, d), lambda i, j: (0, 0))
    in_specs = [row, vec,
                pl.BlockSpec((d, tf), lambda i, j: (0, j)),
                pl.BlockSpec((d, tf), lambda i, j: (0, j)),
                pl.BlockSpec((tf, d), lambda i, j: (j, 0)),
                vec]
    args = [x, pre_g, w1, w3, w2, post_g]
    out_shape = [jax.ShapeDtypeStruct((m, d), F32)]
    out_specs = [row]
    if emit_next:
        in_specs.append(vec)
        args.append(next_g)
        out_shape.append(jax.ShapeDtypeStruct((m, d), BF16))
        out_specs.append(row)
    outs = pl.pallas_call(
        functools.partial(_ffn_kernel, emit_next=emit_next),
        grid=(m // tm, dff // tf),
        in_specs=in_specs,
        out_specs=out_specs,
        out_shape=out_shape,
        scratch_shapes=[pltpu.VMEM((tm, d), BF16), pltpu.VMEM((tm, d), F32)],
        compiler_params=_params(("parallel", "arbitrary")),
        name="ffn",
    )(*args)
    return outs if emit_next else (outs[0], None)


def _softplus(x):
    return jnp.maximum(x, 0.0) + jnp.log1p(jnp.exp(-jnp.abs(x)))


CONV_CARRY = 8
PROJ_EXTRA_INPUTS = {"none": 0, "silu": 0, "sigmoid_bias": 1, "softplus_bias": 1, "conv_silu": 2}


def _proj_kernel(u_ref, w_ref, *rest, epilogue, cast_weight, tiles_per_seq):
    rest = list(rest)
    extra = [rest.pop(0) for _ in range(PROJ_EXTRA_INPUTS[epilogue])]
    o_ref = rest.pop(0)
    i = pl.program_id(1)
    if cast_weight:
        w_sc = rest.pop(0)

        @pl.when(i == 0)
        def _():
            w_sc[...] = w_ref[...].astype(BF16)

        w_ref = w_sc
    acc = _dot_nt(u_ref[...], w_ref[...])
    if epilogue == "silu":
        acc = acc * jax.nn.sigmoid(acc)
    elif epilogue == "sigmoid_bias":
        acc = jax.nn.sigmoid(acc + extra[0][...])
    elif epilogue == "softplus_bias":
        acc = _softplus(acc + extra[0][...])
    elif epilogue == "conv_silu":
        convw_ref, convb_ref = extra
        carry_sc = rest.pop(0)
        tm = acc.shape[0]
        carry = jnp.where(i % tiles_per_seq == 0, 0.0, carry_sc[...])
        carry_sc[...] = acc[tm - CONV_CARRY:]
        ext = jnp.concatenate([carry, acc], axis=0)
        conv = convb_ref[...] + convw_ref[SSD_CONV - 1:SSD_CONV, :] * acc
        for k in range(SSD_CONV - 1):
            start = CONV_CARRY - (SSD_CONV - 1 - k)
            conv = conv + convw_ref[k:k + 1, :] * ext[start:start + tm]
        acc = conv * jax.nn.sigmoid(conv)
    o_ref[...] = acc.astype(o_ref.dtype)


def _proj(u, wt, col_start, n, epilogue, extra=(), *, seq, layer=None, tm=1024, tn=1024, out_dtype=F32):
    m, k = u.shape
    tn = min(tn, n)
    assert col_start % tn == 0 and n % tn == 0 and m % tm == 0 and seq % tm == 0
    cast_weight = wt.dtype != BF16
    col_block = col_start // tn
    if layer is None:
        w_spec = pl.BlockSpec((tn, k), lambda j, i: (col_block + j, 0))
    else:
        w_spec = pl.BlockSpec((None, tn, k), lambda j, i: (layer, col_block + j, 0))
    in_specs = [pl.BlockSpec((tm, k), lambda j, i: (i, 0)), w_spec]
    for e in extra:
        in_specs.append(pl.BlockSpec((e.shape[0], tn), lambda j, i: (0, j)))
    scratch = []
    if cast_weight:
        scratch.append(pltpu.VMEM((tn, k), BF16))
    if epilogue == "conv_silu":
        scratch.append(pltpu.VMEM((CONV_CARRY, tn), F32))
    return pl.pallas_call(
        functools.partial(_proj_kernel, epilogue=epilogue, cast_weight=cast_weight, tiles_per_seq=seq // tm),
        grid=(n // tn, m // tm),
        in_specs=in_specs,
        out_specs=pl.BlockSpec((tm, tn), lambda j, i: (i, j)),
        out_shape=jax.ShapeDtypeStruct((m, n), out_dtype),
        scratch_shapes=scratch,
        compiler_params=_params(("parallel", "arbitrary")),
        name="proj_" + epilogue,
    )(u, wt, *extra)


def _pool_kernel(p_ref, halo_ref, ga_ref, pw_ref, scale_ref, wo_ref, o_ref, mixed_sc, *, tiles_per_seq):
    tm = p_ref.shape[0]
    i = pl.program_id(0)
    tile_in_seq = i % tiles_per_seq
    halo = jnp.where(tile_in_seq == 0, 0.0, halo_ref[...])
    pos = tile_in_seq * tm + lax.broadcasted_iota(jnp.int32, (tm, 1), 0)
    for g, window in enumerate(POOL_WINDOWS):
        cols = slice(g * POOL_GROUP_DIM, (g + 1) * POOL_GROUP_DIM)
        x = p_ref[:, cols]
        s = jnp.concatenate([halo[:, cols], x], axis=0)
        width = 1
        while width < window:
            s = s[width:] + s[:-width]
            width *= 2
        win = s[POOL_HALO - window + 1:]
        inv_cnt = 1.0 / jnp.minimum(pos + 1, window).astype(F32)
        pooled = (win * inv_cnt - x).astype(BF16)
        mixed = _dot(pooled, pw_ref[g]) * scale_ref[:, cols]
        mixed_sc[:, cols] = mixed.astype(BF16)
    o_ref[...] = ga_ref[...] * _dot(mixed_sc[...], wo_ref[...])


def _pool(p, gates, pool_w, pool_scale, w_out, seq, *, tm=256):
    m, d = p.shape
    halo_blocks = tm // POOL_HALO
    row = pl.BlockSpec((tm, d), lambda i: (i, 0))
    return pl.pallas_call(
        functools.partial(_pool_kernel, tiles_per_seq=seq // tm),
        grid=(m // tm,),
        in_specs=[row,
                  pl.BlockSpec((POOL_HALO, d), lambda i: (jnp.maximum(i * halo_blocks - 1, 0), 0)),
                  row,
                  _const_spec(pool_w.shape),
                  _const_spec((1, d)),
                  _const_spec(w_out.shape)],
        out_specs=row,
        out_shape=jax.ShapeDtypeStruct((m, d), F32),
        scratch_shapes=[pltpu.VMEM((tm, d), BF16)],
        compiler_params=_params(("parallel",)),
        name="pool",
    )(p, p, gates, pool_w, pool_scale, w_out)


HEADS_PER_DOT = MXU_DIM // SSD_HEAD_DIM
SSD_CHUNKS_PER_STEP = 4


def _ssd_constants():
    lane = np.arange(SSD_INNER)
    head_of_lane = lane // SSD_HEAD_DIM
    pos_of_lane = lane % SSD_HEAD_DIM
    expand = np.zeros((MXU_DIM, SSD_INNER), np.float32)
    for piece in range(3):
        expand[piece * SSD_HEADS + head_of_lane, lane] = 1.0
    row = np.arange(CHUNK)[:, None]
    diag = (row == pos_of_lane[None, :]).astype(np.float32)
    causal = np.where(row >= pos_of_lane[None, :], 0.0, -np.inf).astype(np.float32)
    blk = np.arange(MXU_DIM) // SSD_HEAD_DIM
    blockdiag = (blk[:, None] == blk[None, :]).astype(np.float32)
    return (jnp.asarray(expand, BF16), jnp.asarray(diag), jnp.asarray(causal), jnp.asarray(blockdiag, BF16))


def _split_select(v, lower_half, pieces):
    p0 = v.astype(BF16).astype(F32)
    r1 = v - p0
    p1 = r1.astype(BF16).astype(F32)
    first = jnp.where(lower_half, p0, p1).astype(BF16)
    if pieces == 2:
        return first
    p2 = (r1 - p1).astype(BF16)
    return jnp.concatenate([first, p2], axis=1)


def _ssd_kernel(xbc_ref, dt_ref, alog_ref, dskip_ref, expand_ref, diag_ref, causal_ref, bd_ref, y_ref,
                state_sc):
    @pl.when(pl.program_id(1) == 0)
    def _():
        state_sc[...] = jnp.zeros_like(state_sc)

    dt = dt_ref[...]
    a = dt * (-jnp.exp(alog_ref[...]) * LOG2_E)
    row_in_chunk = lax.broadcasted_iota(jnp.int32, a.shape, 0) % CHUNK
    lower_half = lax.broadcasted_iota(jnp.int32, a.shape, 1) < SSD_HEADS
    acs = a
    shift = 1
    while shift < CHUNK:
        acs = acs + jnp.where(row_in_chunk >= shift, pltpu.roll(acs, shift, axis=0), 0.0)
        shift *= 2

    colb_all = _dot(_split_select(acs, lower_half, 3), expand_ref[...])
    dtx_all = _dot(_split_select(dt, lower_half, 2), expand_ref[0:LANES, :])

    for q in range(SSD_CHUNKS_PER_STEP):
        rows = slice(q * CHUNK, (q + 1) * CHUNK)
        colb = colb_all[rows]
        xs = xbc_ref[rows, :SSD_INNER]
        b_all = xbc_ref[rows, SSD_INNER:SSD_INNER + SSD_BC].astype(BF16)
        c_all = xbc_ref[rows, SSD_INNER + SSD_BC:].astype(BF16)
        rowb = jnp.sum(colb * diag_ref[...], axis=0, keepdims=True)
        decay_ls = jnp.exp2(colb - rowb + causal_ref[...])
        total = colb[CHUNK - 1:CHUNK, :]
        from_start = jnp.exp2(colb)
        to_end = jnp.exp2(total - colb)
        chunk_decay = jnp.exp2(total)

        xdt = xs * dtx_all[rows]
        xdt_b = xdt.astype(BF16)
        xend_b = (xdt * to_end).astype(BF16)
        skip = dskip_ref[...] * xs

        for g in range(SSD_GROUPS):
            bg = b_all[:, g * SSD_STATE:(g + 1) * SSD_STATE]
            cg = c_all[:, g * SSD_STATE:(g + 1) * SSD_STATE]
            cols = slice(g * SSD_GROUP_WIDTH, (g + 1) * SSD_GROUP_WIDTH)
            cb2 = _dot_nt(cg, jnp.concatenate([bg, bg], axis=0))
            cb = jnp.concatenate([cb2] * (SSD_GROUP_WIDTH // LANES), axis=1)
            gmat = (cb * decay_ls[:, cols]).astype(BF16)
            state = state_sc[g]
            y_g = _dot(cg, state.astype(BF16)) * from_start[:, cols] + skip[:, cols]
            diag_parts = []
            for part in range(SSD_GROUP_WIDTH // MXU_DIM):
                lo = g * SSD_GROUP_WIDTH + part * MXU_DIM
                x_part = xdt_b[:, lo:lo + MXU_DIM]
                x_bd = jnp.concatenate([x_part] * HEADS_PER_DOT, axis=0) * bd_ref[...]
                diag_parts.append(_dot(gmat[:, part * MXU_DIM:(part + 1) * MXU_DIM], x_bd))
            y_ref[rows, cols] = y_g + jnp.concatenate(diag_parts, axis=1)
            new_state = lax.dot_general(bg, xend_b[:, cols], (((0,), (0,)), ((), ())),
                                        preferred_element_type=F32)
            state_sc[g] = state * chunk_decay[:, cols] + new_state


def _ssd(xbc, dt2, a_log2, dskip_x, batch):
    m = xbc.shape[0]
    rows = SSD_CHUNKS_PER_STEP * CHUNK
    steps = m // batch // rows
    assert steps * rows * batch == m
    expand, diag, causal, blockdiag = _ssd_constants()
    tok = lambda width: pl.BlockSpec((rows, width), lambda b, c: (b * steps + c, 0))
    return pl.pallas_call(
        _ssd_kernel,
        grid=(batch, steps),
        in_specs=[tok(SSD_CONV_DIM), tok(LANES), _const_spec(a_log2.shape),
                  _const_spec(dskip_x.shape), _const_spec(expand.shape), _const_spec(diag.shape),
                  _const_spec(causal.shape), _const_spec(blockdiag.shape)],
        out_specs=tok(SSD_INNER),
        out_shape=jax.ShapeDtypeStruct((m, SSD_INNER), F32),
        scratch_shapes=[pltpu.VMEM((SSD_GROUPS, SSD_STATE, SSD_GROUP_WIDTH), F32)],
        compiler_params=_params(("arbitrary", "arbitrary")),
        name="ssd",
    )(xbc, dt2, a_log2, dskip_x, expand, diag, causal, blockdiag)


def _merge_kernel(y_ref, z_ref, ng_ref, wso_ref, gaya_ref, gb_ref, wmix_ref, post_ref, h_ref, next_ref,
                  ho_ref, un_ref, yn_sc):
    yz = y_ref[...] * z_ref[...]
    for g in range(SSD_GROUPS):
        cols = slice(g * SSD_GROUP_WIDTH, (g + 1) * SSD_GROUP_WIDTH)
        yn_sc[:, cols] = _rms(yz[:, cols], ng_ref[:, cols]).astype(BF16)
    y_b = _dot(yn_sc[...], wso_ref[...])
    merged = (gaya_ref[...] + gb_ref[...] * y_b).astype(BF16)
    h = h_ref[...] + _rms(_dot(merged, wmix_ref[...]), post_ref[...])
    ho_ref[...] = h
    un_ref[...] = _rms(h, next_ref[...]).astype(BF16)


def _merge(y, z, norm_g, w_ssd_out, gaya, gates, w_mix, post_g, h, next_g, *, tm=128):
    m, d = h.shape
    wide = pl.BlockSpec((tm, SSD_INNER), lambda i: (i, 0))
    row = pl.BlockSpec((tm, d), lambda i: (i, 0))
    return pl.pallas_call(
        _merge_kernel,
        grid=(m // tm,),
        in_specs=[wide, wide, _const_spec((1, SSD_INNER)), _const_spec(w_ssd_out.shape),
                  row,
                  pl.BlockSpec((tm, d), lambda i: (i, 1)),
                  _const_spec(w_mix.shape), _const_spec((1, d)), row, _const_spec((1, d))],
        out_specs=[row, row],
        out_shape=[jax.ShapeDtypeStruct((m, d), F32), jax.ShapeDtypeStruct((m, d), BF16)],
        scratch_shapes=[pltpu.VMEM((tm, SSD_INNER), BF16)],
        compiler_params=_params(("parallel",)),
        name="merge",
    )(y, z, norm_g, w_ssd_out, gaya, gates, w_mix, post_g, h, next_g)


def _kv_kernel(mem_ref, g_ref, wk_ref, wv_ref, k_ref, v_ref):
    mem_n = _rms(mem_ref[...], g_ref[...]).astype(BF16)
    k_ref[...] = _dot(mem_n, wk_ref[...]).astype(BF16)
    v_ref[...] = _dot(mem_n, wv_ref[...]).astype(BF16)


def _kv(mem, g, w_k, w_v):
    rows = mem.shape[0]
    out = jax.ShapeDtypeStruct((rows, X_WIDTH), BF16)
    return pl.pallas_call(_kv_kernel, out_shape=[out, out], name="kv",
                          compiler_params=pltpu.CompilerParams(vmem_limit_bytes=V7X_VMEM_LIMIT))(mem, g, w_k, w_v)


def _attn_kernel(u_ref, h_ref, wq_ref, k_ref, v_ref, wo_ref, post_ref, ho_ref, o_sc):
    q = _dot(u_ref[...], wq_ref[...]).astype(BF16)
    for hd in range(X_HEADS):
        cols = slice(hd * X_HEAD_DIM, (hd + 1) * X_HEAD_DIM)
        s = _dot_nt(q[:, cols], k_ref[:, cols]) * (1.0 / math.sqrt(X_HEAD_DIM))
        e = jnp.exp(s - jnp.max(s, axis=-1, keepdims=True))
        p = (e / jnp.sum(e, axis=-1, keepdims=True)).astype(BF16)
        o_sc[:, cols] = _dot(p, v_ref[:, cols]).astype(BF16)
    ho_ref[...] = h_ref[...] + _rms(_dot(o_sc[...], wo_ref[...]), post_ref[...])


def _attn(u, h, w_q, k, v, w_o, post_g, seq, *, tm=512):
    m, d = h.shape
    tiles_per_seq = seq // tm
    row = pl.BlockSpec((tm, d), lambda i: (i, 0))
    mem = pl.BlockSpec((N_MEM, X_WIDTH), lambda i: (i // tiles_per_seq, 0))
    return pl.pallas_call(
        _attn_kernel,
        grid=(m // tm,),
        in_specs=[row, row, _const_spec(w_q.shape), mem, mem, _const_spec(w_o.shape),
                  _const_spec((1, d))],
        out_specs=row,
        out_shape=jax.ShapeDtypeStruct((m, d), F32),
        scratch_shapes=[pltpu.VMEM((tm, X_WIDTH), BF16)],
        compiler_params=_params(("parallel",)),
        name="attn",
    )(u, h, w_q, k, v, w_o, post_g)


def kernel(x, mem, ffn1_pre_g, ffn1_w1, ffn1_w3, ffn1_w2, ffn1_post_g, mix_pre_g, w_in, b_gate, pool_w, pool_scale, w_pool_out, conv_w, conv_b, dt_bias, a_log, d_skip, ssd_norm_g, w_ssd_out, w_mix_out, mix_post_g, xattn_pre_g, mem_norm_g, w_q, w_k, w_v, w_xo, xattn_post_g, ffn2_pre_g, ffn2_w1, ffn2_w3, ffn2_w2, ffn2_post_g):
    batch, seq, d = x.shape
    depth = ffn1_w1.shape[0]
    c0 = D_MODEL
    c1 = c0 + SSD_INNER
    c2 = c1 + SSD_CONV_DIM
    c3 = c2 + SSD_HEADS
    vec = lambda g: g.reshape(1, -1).astype(F32)
    twice = lambda t: jnp.concatenate([t, t], axis=-1)

    h = x.reshape(batch * seq, d)
    mem2 = mem.reshape(batch * N_MEM, d)
    w_in_t = jnp.swapaxes(w_in, 1, 2)
    u = None
    for i in range(depth):
        bf = lambda w: w[i].astype(BF16)

        h, u = _ffn(h, vec(ffn1_pre_g[i]), bf(ffn1_w1), bf(ffn1_w3), bf(ffn1_w2), vec(ffn1_post_g[i]),
                    vec(mix_pre_g[i]))

        p_in = _proj(u, w_in_t, 0, c0, "none", seq=seq, layer=i)
        z_act = _proj(u, w_in_t, c0, c1 - c0, "silu", seq=seq, layer=i)
        xbc = _proj(u, w_in_t, c1, c2 - c1, "conv_silu", (conv_w[i], vec(conv_b[i])), seq=seq, layer=i, tn=512)
        w_dt_t = w_in_t[i, c2:c3].astype(BF16)
        dt2 = _proj(u, jnp.concatenate([w_dt_t, w_dt_t], axis=0), 0, 2 * SSD_HEADS, "softplus_bias",
                    (twice(vec(dt_bias[i])),), seq=seq)
        gates = _proj(u, w_in_t[i, c3:].astype(BF16), 0, 2 * D_MODEL, "sigmoid_bias", (vec(b_gate[i]),),
                      seq=seq)

        gaya = _pool(p_in, gates, bf(pool_w), vec(pool_scale[i]), bf(w_pool_out), seq)
        y = _ssd(xbc, dt2, twice(vec(a_log[i])), jnp.repeat(vec(d_skip[i]), SSD_HEAD_DIM, axis=1), batch)
        h, u = _merge(y, z_act, vec(ssd_norm_g[i]), bf(w_ssd_out), gaya, gates, bf(w_mix_out),
                      vec(mix_post_g[i]), h, vec(xattn_pre_g[i]))

        k, v = _kv(mem2, vec(mem_norm_g[i]), bf(w_k), bf(w_v))
        h = _attn(u, h, bf(w_q), k, v, bf(w_xo), vec(xattn_post_g[i]), seq)

        last = i == depth - 1
        h, u = _ffn(h, vec(ffn2_pre_g[i]), bf(ffn2_w1), bf(ffn2_w3), bf(ffn2_w2), vec(ffn2_post_g[i]),
                    None if last else vec(ffn1_pre_g[i + 1]))
    return h.reshape(batch, seq, d)
```

```python
import functools
import math

import jax
import jax.numpy as jnp
import numpy as np
from jax import lax
from jax.experimental import pallas as pl
from jax.experimental.pallas import tpu as pltpu

F32 = jnp.float32
BF16 = jnp.bfloat16

D_MODEL = 2048
D_FF = 5632
CHUNK = 64
N_MEM = 256
POOL_WINDOWS = (2, 4, 8, 16)
POOL_GROUP_DIM = D_MODEL // len(POOL_WINDOWS)
POOL_HALO = 16
SSD_INNER = 4096
SSD_HEAD_DIM = 64
SSD_HEADS = SSD_INNER // SSD_HEAD_DIM
SSD_GROUPS = 8
SSD_STATE = 128
SSD_GROUP_WIDTH = SSD_INNER // SSD_GROUPS
SSD_BC = SSD_GROUPS * SSD_STATE
SSD_CONV = 4
SSD_CONV_DIM = SSD_INNER + 2 * SSD_BC
X_HEADS = 4
X_HEAD_DIM = 128
X_WIDTH = X_HEADS * X_HEAD_DIM
EPS = 1e-6
LOG2_E = math.log2(math.e)

LANES = 128
F32_SUBLANES = 8
BF16_SUBLANES = 16
MXU_DIM = 256
V7X_VMEM_LIMIT = 60 * 1024 * 1024


def _params(semantics, flags=None):
    return pltpu.CompilerParams(dimension_semantics=semantics, vmem_limit_bytes=V7X_VMEM_LIMIT, flags=flags)


def _rms(x, g):
    return x * lax.rsqrt(jnp.mean(x * x, axis=-1, keepdims=True) + EPS) * g


def _dot(a, b):
    return jnp.dot(a, b, preferred_element_type=F32)


def _dot_nt(a, b):
    return lax.dot_general(a, b, (((1,), (1,)), ((), ())), preferred_element_type=F32)


def _const_spec(shape):
    return pl.BlockSpec(shape, lambda *_: (0,) * len(shape), pipeline_mode=pl.Buffered(1))


def _ffn_kernel(x_ref, pre_ref, w1_ref, w3_ref, w2_ref, post_ref, *rest, emit_next):
    if emit_next:
        next_ref, h_ref, un_ref, u_sc, acc_sc = rest
    else:
        h_ref, u_sc, acc_sc = rest
    j = pl.program_id(1)

    @pl.when(j == 0)
    def _():
        u_sc[...] = _rms(x_ref[...], pre_ref[...]).astype(BF16)
        acc_sc[...] = jnp.zeros_like(acc_sc)

    u = u_sc[...]
    a = _dot(u, w1_ref[...])
    b = _dot(u, w3_ref[...])
    act = (a * jax.nn.sigmoid(a) * b).astype(BF16)
    acc_sc[...] += _dot(act, w2_ref[...])

    @pl.when(j == pl.num_programs(1) - 1)
    def _():
        h = x_ref[...] + 0.5 * _rms(acc_sc[...], post_ref[...])
        h_ref[...] = h
        if emit_next:
            un_ref[...] = _rms(h, next_ref[...]).astype(BF16)


def _ffn(x, pre_g, w1, w3, w2, post_g, next_g, *, tm=512, tf=512):
    m, d = x.shape
    dff = w1.shape[1]
    emit_next = next_g is not None
    row = pl.BlockSpec((tm, d), lambda i, j: (i, 0))
    vec = pl.BlockSpec((1, d), lambda i, j: (0, 0))
    in_specs = [row, vec,
                pl.BlockSpec((d, tf), lambda i, j: (0, j)),
                pl.BlockSpec((d, tf), lambda i, j: (0, j)),
                pl.BlockSpec((tf, d), lambda i, j: (j, 0)),
                vec]
    args = [x, pre_g, w1, w3, w2, post_g]
    out_shape = [jax.ShapeDtypeStruct((m, d), F32)]
    out_specs = [row]
    if emit_next:
        in_specs.append(vec)
        args.append(next_g)
        out_shape.append(jax.ShapeDtypeStruct((m, d), BF16))
        out_specs.append(row)
    outs = pl.pallas_call(
        functools.partial(_ffn_kernel, emit_next=emit_next),
        grid=(m // tm, dff // tf),
        in_specs=in_specs,
        out_specs=out_specs,
        out_shape=out_shape,
        scratch_shapes=[pltpu.VMEM((tm, d), BF16), pltpu.VMEM((tm, d), F32)],
        compiler_params=_params(("parallel", "arbitrary")),
        name="ffn",
    )(*args)
    return outs if emit_next else (outs[0], None)


def _softplus(x):
    return jnp.maximum(x, 0.0) + jnp.log1p(jnp.exp(-jnp.abs(x)))


CONV_CARRY = 8
PROJ_EXTRA_INPUTS = {"none": 0, "silu": 0, "sigmoid_bias": 1, "softplus_bias": 1, "conv_silu": 2}


def _proj_kernel(u_ref, w_ref, *rest, epilogue, tiles_per_seq, n_cast):
    rest = list(rest)
    extra = [rest.pop(0) for _ in range(PROJ_EXTRA_INPUTS[epilogue])]
    cast_in = [rest.pop(0) for _ in range(n_cast)]
    o_ref = rest.pop(0)
    cast_out = [rest.pop(0) for _ in range(n_cast)]
    w_sc = rest.pop(0)
    i = pl.program_id(1)

    @pl.when(i == 0)
    def _():
        w = w_ref[...].reshape(w_ref.shape[-2:]).astype(BF16)
        rows = w.shape[0]
        for r in range(w_sc.shape[0] // rows):
            w_sc[r * rows:(r + 1) * rows, :] = w

    for src, dst in zip(cast_in, cast_out):
        dst[...] = src[...].astype(BF16)

    acc = _dot_nt(u_ref[...], w_sc[...])
    if epilogue == "silu":
        acc = acc * jax.nn.sigmoid(acc)
    elif epilogue == "sigmoid_bias":
        acc = jax.nn.sigmoid(acc + extra[0][...])
    elif epilogue == "softplus_bias":
        acc = _softplus(acc + extra[0][...])
    elif epilogue == "conv_silu":
        convw_ref, convb_ref = extra
        carry_sc = rest.pop(0)
        tm = acc.shape[0]
        carry = jnp.where(i % tiles_per_seq == 0, 0.0, carry_sc[...])
        carry_sc[...] = acc[tm - CONV_CARRY:]
        ext = jnp.concatenate([carry, acc], axis=0)
        conv = convb_ref[...] + convw_ref[SSD_CONV - 1:SSD_CONV, :] * acc
        for k in range(SSD_CONV - 1):
            start = CONV_CARRY - (SSD_CONV - 1 - k)
            conv = conv + convw_ref[k:k + 1, :] * ext[start:start + tm]
        acc = conv * jax.nn.sigmoid(conv)
    o_ref[...] = acc.astype(o_ref.dtype)


def _proj(u, wt, col_start, n, epilogue, extra=(), to_cast=(), *, seq, layer, tm=1024, tn=1024, repeat=1,
          out_dtype=F32):
    m, k = u.shape
    tn = min(tn, n)
    assert n % tn == 0 and m % tm == 0 and seq % tm == 0 and (repeat == 1 or n == tn)
    steps_j, steps_i = n // tn, m // tm
    if col_start % tn == 0:
        col_block = col_start // tn
        w_spec = pl.BlockSpec((None, tn, k), lambda j, i: (layer, col_block + j, 0))
    else:
        assert col_start % F32_SUBLANES == 0 and tn % F32_SUBLANES == 0
        w_spec = pl.BlockSpec((pl.Element(1), pl.Element(tn), pl.Element(k)),
                              lambda j, i: (layer, pl.multiple_of(col_start + j * tn, F32_SUBLANES), 0))
    to = tn * repeat
    in_specs = [pl.BlockSpec((tm, k), lambda j, i: (i, 0)), w_spec]
    for e in extra:
        in_specs.append(pl.BlockSpec((e.shape[0], to), lambda j, i: (0, j)))
    out_specs = [pl.BlockSpec((tm, to), lambda j, i: (i, j))]
    out_shape = [jax.ShapeDtypeStruct((m, n * repeat), out_dtype)]
    for arr in to_cast:
        rows = arr.shape[0] // (steps_j * steps_i)
        assert rows * steps_j * steps_i == arr.shape[0] and rows % BF16_SUBLANES == 0, arr.shape
        spec = pl.BlockSpec((rows, arr.shape[1]), lambda j, i: (j * steps_i + i, 0))
        in_specs.append(spec)
        out_specs.append(spec)
        out_shape.append(jax.ShapeDtypeStruct(arr.shape, BF16))
    scratch = [pltpu.VMEM((to, k), BF16)]
    if epilogue == "conv_silu":
        scratch.append(pltpu.VMEM((CONV_CARRY, to), F32))
    outs = pl.pallas_call(
        functools.partial(_proj_kernel, epilogue=epilogue, tiles_per_seq=seq // tm, n_cast=len(to_cast)),
        grid=(steps_j, steps_i),
        in_specs=in_specs,
        out_specs=out_specs,
        out_shape=out_shape,
        scratch_shapes=scratch,
        compiler_params=_params(("parallel", "arbitrary")),
        name="proj_" + epilogue,
    )(u, wt, *extra, *to_cast)
    return outs if to_cast else outs[0]


def _pool_kernel(p_ref, halo_ref, ga_ref, pw_ref, scale_ref, wo_ref, o_ref, mixed_sc, *, tiles_per_seq):
    tm = p_ref.shape[0]
    i = pl.program_id(0)
    tile_in_seq = i % tiles_per_seq
    halo = jnp.where(tile_in_seq == 0, 0.0, halo_ref[...])
    pos = tile_in_seq * tm + lax.broadcasted_iota(jnp.int32, (tm, 1), 0)
    for g, window in enumerate(POOL_WINDOWS):
        cols = slice(g * POOL_GROUP_DIM, (g + 1) * POOL_GROUP_DIM)
        x = p_ref[:, cols]
        s = jnp.concatenate([halo[:, cols], x], axis=0)
        width = 1
        while width < window:
            s = s[width:] + s[:-width]
            width *= 2
        win = s[POOL_HALO - window + 1:]
        inv_cnt = 1.0 / jnp.minimum(pos + 1, window).astype(F32)
        pooled = (win * inv_cnt - x).astype(BF16)
        mixed = _dot(pooled, pw_ref[g]) * scale_ref[:, cols]
        mixed_sc[:, cols] = mixed.astype(BF16)
    o_ref[...] = ga_ref[...] * _dot(mixed_sc[...], wo_ref[...])


def _pool(p, gates, pool_w, pool_scale, w_out, seq, *, tm=512):
    m, d = p.shape
    halo_blocks = tm // POOL_HALO
    row = pl.BlockSpec((tm, d), lambda i: (i, 0))
    return pl.pallas_call(
        functools.partial(_pool_kernel, tiles_per_seq=seq // tm),
        grid=(m // tm,),
        in_specs=[row,
                  pl.BlockSpec((POOL_HALO, d), lambda i: (jnp.maximum(i * halo_blocks - 1, 0), 0)),
                  row,
                  _const_spec(pool_w.shape),
                  _const_spec((1, d)),
                  _const_spec(w_out.shape)],
        out_specs=row,
        out_shape=jax.ShapeDtypeStruct((m, d), F32),
        scratch_shapes=[pltpu.VMEM((tm, d), BF16)],
        compiler_params=_params(("parallel",)),
        name="pool",
    )(p, p, gates, pool_w, pool_scale, w_out)


HEADS_PER_DOT = MXU_DIM // SSD_HEAD_DIM
SSD_CHUNKS_PER_STEP = 4


def _ssd_constants():
    lane = np.arange(SSD_INNER)
    head_of_lane = lane // SSD_HEAD_DIM
    pos_of_lane = lane % SSD_HEAD_DIM
    expand = np.zeros((MXU_DIM, SSD_INNER), np.float32)
    for piece in range(3):
        expand[piece * SSD_HEADS + head_of_lane, lane] = 1.0
    row = np.arange(CHUNK)[:, None]
    diag = (row == pos_of_lane[None, :]).astype(np.float32)
    causal = np.where(row >= pos_of_lane[None, :], 0.0, -np.inf).astype(np.float32)
    blk = np.arange(MXU_DIM) // SSD_HEAD_DIM
    blockdiag = (blk[:, None] == blk[None, :]).astype(np.float32)
    return (jnp.asarray(expand, BF16), jnp.asarray(diag), jnp.asarray(causal), jnp.asarray(blockdiag, BF16))


def _split_select(v, lower_half, pieces):
    p0 = v.astype(BF16).astype(F32)
    r1 = v - p0
    p1 = r1.astype(BF16).astype(F32)
    first = jnp.where(lower_half, p0, p1).astype(BF16)
    if pieces == 2:
        return first
    p2 = (r1 - p1).astype(BF16)
    return jnp.concatenate([first, p2], axis=1)


def _ssd_kernel(xbc_ref, dt_ref, alog_ref, dskip_ref, expand_ref, diag_ref, causal_ref, bd_ref, y_ref,
                state_sc):
    @pl.when(pl.program_id(1) == 0)
    def _():
        state_sc[...] = jnp.zeros_like(state_sc)

    dt = dt_ref[...]
    a = dt * (-jnp.exp(alog_ref[...]) * LOG2_E)
    row_in_chunk = lax.broadcasted_iota(jnp.int32, a.shape, 0) % CHUNK
    lower_half = lax.broadcasted_iota(jnp.int32, a.shape, 1) < SSD_HEADS
    acs = a
    shift = 1
    while shift < CHUNK:
        acs = acs + jnp.where(row_in_chunk >= shift, pltpu.roll(acs, shift, axis=0), 0.0)
        shift *= 2

    colb_all = _dot(_split_select(acs, lower_half, 3), expand_ref[...])
    dtx_all = _dot(_split_select(dt, lower_half, 2), expand_ref[0:LANES, :])

    for q in range(SSD_CHUNKS_PER_STEP):
        rows = slice(q * CHUNK, (q + 1) * CHUNK)
        colb = colb_all[rows]
        xs = xbc_ref[rows, :SSD_INNER]
        b_all = xbc_ref[rows, SSD_INNER:SSD_INNER + SSD_BC].astype(BF16)
        c_all = xbc_ref[rows, SSD_INNER + SSD_BC:].astype(BF16)
        rowb = jnp.sum(colb * diag_ref[...], axis=0, keepdims=True)
        decay_ls = jnp.exp2(colb - rowb + causal_ref[...])
        total = colb[CHUNK - 1:CHUNK, :]
        from_start = jnp.exp2(colb)
        to_end = jnp.exp2(total - colb)
        chunk_decay = jnp.exp2(total)

        xdt = xs * dtx_all[rows]
        xdt_b = xdt.astype(BF16)
        xend_b = (xdt * to_end).astype(BF16)
        skip = dskip_ref[...] * xs

        for g in range(SSD_GROUPS):
            bg = b_all[:, g * SSD_STATE:(g + 1) * SSD_STATE]
            cg = c_all[:, g * SSD_STATE:(g + 1) * SSD_STATE]
            cols = slice(g * SSD_GROUP_WIDTH, (g + 1) * SSD_GROUP_WIDTH)
            cb2 = _dot_nt(cg, jnp.concatenate([bg, bg], axis=0))
            cb = jnp.concatenate([cb2] * (SSD_GROUP_WIDTH // LANES), axis=1)
            gmat = (cb * decay_ls[:, cols]).astype(BF16)
            state = state_sc[g]
            y_g = _dot(cg, state.astype(BF16)) * from_start[:, cols] + skip[:, cols]
            diag_parts = []
            for part in range(SSD_GROUP_WIDTH // MXU_DIM):
                lo = g * SSD_GROUP_WIDTH + part * MXU_DIM
                x_part = xdt_b[:, lo:lo + MXU_DIM]
                x_bd = jnp.concatenate([x_part] * HEADS_PER_DOT, axis=0) * bd_ref[...]
                diag_parts.append(_dot(gmat[:, part * MXU_DIM:(part + 1) * MXU_DIM], x_bd))
            y_ref[rows, cols] = y_g + jnp.concatenate(diag_parts, axis=1)
            new_state = lax.dot_general(bg, xend_b[:, cols], (((0,), (0,)), ((), ())),
                                        preferred_element_type=F32)
            state_sc[g] = state * chunk_decay[:, cols] + new_state


def _ssd(xbc, dt2, a_log2, dskip_x, batch):
    m = xbc.shape[0]
    rows = SSD_CHUNKS_PER_STEP * CHUNK
    steps = m // batch // rows
    assert steps * rows * batch == m
    expand, diag, causal, blockdiag = _ssd_constants()
    tok = lambda width: pl.BlockSpec((rows, width), lambda b, c: (b * steps + c, 0))
    return pl.pallas_call(
        _ssd_kernel,
        grid=(batch, steps),
        in_specs=[tok(SSD_CONV_DIM), tok(LANES), _const_spec(a_log2.shape),
                  _const_spec(dskip_x.shape), _const_spec(expand.shape), _const_spec(diag.shape),
                  _const_spec(causal.shape), _const_spec(blockdiag.shape)],
        out_specs=tok(SSD_INNER),
        out_shape=jax.ShapeDtypeStruct((m, SSD_INNER), F32),
        scratch_shapes=[pltpu.VMEM((SSD_GROUPS, SSD_STATE, SSD_GROUP_WIDTH), F32)],
        compiler_params=_params(("arbitrary", "arbitrary")),
        name="ssd",
    )(xbc, dt2, a_log2, dskip_x, expand, diag, causal, blockdiag)


def _merge_kernel(y_ref, z_ref, ng_ref, wso_ref, gaya_ref, gb_ref, wmix_ref, post_ref, h_ref, next_ref,
                  ho_ref, un_ref, yn_sc):
    yz = y_ref[...] * z_ref[...]
    for g in range(SSD_GROUPS):
        cols = slice(g * SSD_GROUP_WIDTH, (g + 1) * SSD_GROUP_WIDTH)
        yn_sc[:, cols] = _rms(yz[:, cols], ng_ref[:, cols]).astype(BF16)
    y_b = _dot(yn_sc[...], wso_ref[...])
    merged = (gaya_ref[...] + gb_ref[...] * y_b).astype(BF16)
    h = h_ref[...] + _rms(_dot(merged, wmix_ref[...]), post_ref[...])
    ho_ref[...] = h
    un_ref[...] = _rms(h, next_ref[...]).astype(BF16)


def _merge(y, z, norm_g, w_ssd_out, gaya, gates, w_mix, post_g, h, next_g, *, tm=128):
    m, d = h.shape
    wide = pl.BlockSpec((tm, SSD_INNER), lambda i: (i, 0))
    row = pl.BlockSpec((tm, d), lambda i: (i, 0))
    return pl.pallas_call(
        _merge_kernel,
        grid=(m // tm,),
        in_specs=[wide, wide, _const_spec((1, SSD_INNER)), _const_spec(w_ssd_out.shape),
                  row,
                  pl.BlockSpec((tm, d), lambda i: (i, 1)),
                  _const_spec(w_mix.shape), _const_spec((1, d)), row, _const_spec((1, d))],
        out_specs=[row, row],
        out_shape=[jax.ShapeDtypeStruct((m, d), F32), jax.ShapeDtypeStruct((m, d), BF16)],
        scratch_shapes=[pltpu.VMEM((tm, SSD_INNER), BF16)],
        compiler_params=_params(("parallel",)),
        name="merge",
    )(y, z, norm_g, w_ssd_out, gaya, gates, w_mix, post_g, h, next_g)


def _kv_kernel(mem_ref, g_ref, wk_ref, wv_ref, k_ref, v_ref):
    mem_n = _rms(mem_ref[...], g_ref[...]).astype(BF16)
    k_ref[...] = _dot(mem_n, wk_ref[...]).astype(BF16)
    v_ref[...] = _dot(mem_n, wv_ref[...]).astype(BF16)


def _kv(mem, g, w_k, w_v):
    rows = mem.shape[0]
    out = jax.ShapeDtypeStruct((rows, X_WIDTH), BF16)
    return pl.pallas_call(_kv_kernel, out_shape=[out, out], name="kv",
                          compiler_params=pltpu.CompilerParams(vmem_limit_bytes=V7X_VMEM_LIMIT))(mem, g, w_k, w_v)


def _attn_kernel(u_ref, h_ref, wq_ref, k_ref, v_ref, wo_ref, post_ref, ho_ref, o_sc):
    q = _dot(u_ref[...], wq_ref[...]).astype(BF16)
    for hd in range(X_HEADS):
        cols = slice(hd * X_HEAD_DIM, (hd + 1) * X_HEAD_DIM)
        s = _dot_nt(q[:, cols], k_ref[:, cols]) * (1.0 / math.sqrt(X_HEAD_DIM))
        e = jnp.exp(s - jnp.max(s, axis=-1, keepdims=True))
        p = (e / jnp.sum(e, axis=-1, keepdims=True)).astype(BF16)
        o_sc[:, cols] = _dot(p, v_ref[:, cols]).astype(BF16)
    ho_ref[...] = h_ref[...] + _rms(_dot(o_sc[...], wo_ref[...]), post_ref[...])


def _attn(u, h, w_q, k, v, w_o, post_g, seq, *, tm=512):
    m, d = h.shape
    tiles_per_seq = seq // tm
    row = pl.BlockSpec((tm, d), lambda i: (i, 0))
    mem = pl.BlockSpec((N_MEM, X_WIDTH), lambda i: (i // tiles_per_seq, 0))
    return pl.pallas_call(
        _attn_kernel,
        grid=(m // tm,),
        in_specs=[row, row, _const_spec(w_q.shape), mem, mem, _const_spec(w_o.shape),
                  _const_spec((1, d))],
        out_specs=row,
        out_shape=jax.ShapeDtypeStruct((m, d), F32),
        scratch_shapes=[pltpu.VMEM((tm, X_WIDTH), BF16)],
        compiler_params=_params(("parallel",)),
        name="attn",
    )(u, h, w_q, k, v, w_o, post_g)


def kernel(x, mem, ffn1_pre_g, ffn1_w1, ffn1_w3, ffn1_w2, ffn1_post_g, mix_pre_g, w_in, b_gate, pool_w, pool_scale, w_pool_out, conv_w, conv_b, dt_bias, a_log, d_skip, ssd_norm_g, w_ssd_out, w_mix_out, mix_post_g, xattn_pre_g, mem_norm_g, w_q, w_k, w_v, w_xo, xattn_post_g, ffn2_pre_g, ffn2_w1, ffn2_w3, ffn2_w2, ffn2_post_g):
    batch, seq, d = x.shape
    depth = ffn1_w1.shape[0]
    c0 = D_MODEL
    c1 = c0 + SSD_INNER
    c2 = c1 + SSD_CONV_DIM
    c3 = c2 + SSD_HEADS
    vec = lambda g: g.reshape(1, -1).astype(F32)
    twice = lambda t: jnp.concatenate([t, t], axis=-1)

    h = x.reshape(batch * seq, d)
    mem2 = mem.reshape(batch * N_MEM, d)
    w_in_t = jnp.swapaxes(w_in, 1, 2)
    u = None
    for i in range(depth):
        bf = lambda w: w[i].astype(BF16)

        h, u = _ffn(h, vec(ffn1_pre_g[i]), bf(ffn1_w1), bf(ffn1_w3), bf(ffn1_w2), vec(ffn1_post_g[i]),
                    vec(mix_pre_g[i]))

        p_in, wb_q, wb_k, wb_v, wb_xo = _proj(u, w_in_t, 0, c0, "none", (), (w_q[i], w_k[i], w_v[i], w_xo[i]),
                                              seq=seq, layer=i)
        z_act, f2_w1, f2_w3, f2_w2 = _proj(u, w_in_t, c0, c1 - c0, "silu", (),
                                           (ffn2_w1[i], ffn2_w3[i], ffn2_w2[i]), seq=seq, layer=i)
        xbc = _proj(u, w_in_t, c1, c2 - c1, "conv_silu", (conv_w[i], vec(conv_b[i])), seq=seq, layer=i, tn=512)
        dt2 = _proj(u, w_in_t, c2, c3 - c2, "softplus_bias", (twice(vec(dt_bias[i])),), seq=seq, layer=i,
                    repeat=2)
        gates, wb_ssd_out, wb_mix_out, wb_pool_out, wb_pool = _proj(
            u, w_in_t, c3, 2 * D_MODEL, "sigmoid_bias", (vec(b_gate[i]),),
            (w_ssd_out[i], w_mix_out[i], w_pool_out[i], pool_w[i].reshape(-1, POOL_GROUP_DIM)),
            seq=seq, layer=i)

        gaya = _pool(p_in, gates, wb_pool.reshape(pool_w.shape[1:]), vec(pool_scale[i]), wb_pool_out, seq)
        y = _ssd(xbc, dt2, twice(vec(a_log[i])), jnp.repeat(vec(d_skip[i]), SSD_HEAD_DIM, axis=1), batch)
        h, u = _merge(y, z_act, vec(ssd_norm_g[i]), wb_ssd_out, gaya, gates, wb_mix_out,
                      vec(mix_post_g[i]), h, vec(xattn_pre_g[i]))

        k, v = _kv(mem2, vec(mem_norm_g[i]), wb_k, wb_v)
        h = _attn(u, h, wb_q, k, v, wb_xo, vec(xattn_post_g[i]), seq)

        last = i == depth - 1
        h, u = _ffn(h, vec(ffn2_pre_g[i]), f2_w1, f2_w3, f2_w2, vec(ffn2_post_g[i]),
                    None if last else vec(ffn1_pre_g[i + 1]))
    return h.reshape(batch, seq, d)
```

```python
import functools
import math

import jax
import jax.numpy as jnp
import numpy as np
from jax import lax
from jax.experimental import pallas as pl
from jax.experimental.pallas import tpu as pltpu

F32 = jnp.float32
BF16 = jnp.bfloat16

D_MODEL = 2048
D_FF = 5632
CHUNK = 64
N_MEM = 256
POOL_WINDOWS = (2, 4, 8, 16)
POOL_GROUP_DIM = D_MODEL // len(POOL_WINDOWS)
POOL_HALO = 16
SSD_INNER = 4096
SSD_HEAD_DIM = 64
SSD_HEADS = SSD_INNER // SSD_HEAD_DIM
SSD_GROUPS = 8
SSD_STATE = 128
SSD_GROUP_WIDTH = SSD_INNER // SSD_GROUPS
SSD_BC = SSD_GROUPS * SSD_STATE
SSD_CONV = 4
SSD_CONV_DIM = SSD_INNER + 2 * SSD_BC
X_HEADS = 4
X_HEAD_DIM = 128
X_WIDTH = X_HEADS * X_HEAD_DIM
EPS = 1e-6
LOG2_E = math.log2(math.e)

LANES = 128
F32_SUBLANES = 8
BF16_SUBLANES = 16
MXU_DIM = 256
V7X_VMEM_LIMIT = 60 * 1024 * 1024


def _params(semantics, flags=None):
    return pltpu.CompilerParams(dimension_semantics=semantics, vmem_limit_bytes=V7X_VMEM_LIMIT, flags=flags)


def _rms(x, g):
    return x * lax.rsqrt(jnp.mean(x * x, axis=-1, keepdims=True) + EPS) * g


def _dot(a, b):
    return jnp.dot(a, b, preferred_element_type=F32)


def _dot_nt(a, b):
    return lax.dot_general(a, b, (((1,), (1,)), ((), ())), preferred_element_type=F32)


def _const_spec(shape):
    return pl.BlockSpec(shape, lambda *_: (0,) * len(shape), pipeline_mode=pl.Buffered(1))


def _ffn_kernel(x_ref, pre_ref, w1_ref, w3_ref, w2_ref, post_ref, *rest, emit_next):
    if emit_next:
        next_ref, h_ref, un_ref, u_sc, acc_sc = rest
    else:
        h_ref, u_sc, acc_sc = rest
    j = pl.program_id(1)

    @pl.when(j == 0)
    def _():
        u_sc[...] = _rms(x_ref[...], pre_ref[...]).astype(BF16)
        acc_sc[...] = jnp.zeros_like(acc_sc)

    u = u_sc[...]
    a = _dot(u, w1_ref[...])
    b = _dot(u, w3_ref[...])
    act = (a * jax.nn.sigmoid(a) * b).astype(BF16)
    acc_sc[...] += _dot(act, w2_ref[...])

    @pl.when(j == pl.num_programs(1) - 1)
    def _():
        h = x_ref[...] + 0.5 * _rms(acc_sc[...], post_ref[...])
        h_ref[...] = h
        if emit_next:
            un_ref[...] = _rms(h, next_ref[...]).astype(BF16)


def _ffn(x, pre_g, w1, w3, w2, post_g, next_g, *, tm=512, tf=512):
    m, d = x.shape
    dff = w1.shape[1]
    emit_next = next_g is not None
    row = pl.BlockSpec((tm, d), lambda i, j: (i, 0))
    vec = pl.BlockSpec((1, d), lambda i, j: (0, 0))
    in_specs = [row, vec,
                pl.BlockSpec((d, tf), lambda i, j: (0, j)),
                pl.BlockSpec((d, tf), lambda i, j: (0, j)),
                pl.BlockSpec((tf, d), lambda i, j: (j, 0)),
                vec]
    args = [x, pre_g, w1, w3, w2, post_g]
    out_shape = [jax.ShapeDtypeStruct((m, d), F32)]
    out_specs = [row]
    if emit_next:
        in_specs.append(vec)
        args.append(next_g)
        out_shape.append(jax.ShapeDtypeStruct((m, d), BF16))
        out_specs.append(row)
    outs = pl.pallas_call(
        functools.partial(_ffn_kernel, emit_next=emit_next),
        grid=(m // tm, dff // tf),
        in_specs=in_specs,
        out_specs=out_specs,
        out_shape=out_shape,
        scratch_shapes=[pltpu.VMEM((tm, d), BF16), pltpu.VMEM((tm, d), F32)],
        compiler_params=_params(("parallel", "arbitrary")),
        name="ffn",
    )(*args)
    return outs if emit_next else (outs[0], None)


def _softplus(x):
    return jnp.maximum(x, 0.0) + jnp.log1p(jnp.exp(-jnp.abs(x)))


CONV_CARRY = 8
PROJ_EXTRA_INPUTS = {"none": 0, "silu": 0, "sigmoid_bias": 1, "softplus_bias": 1, "conv_silu": 2}


def _proj_kernel(u_ref, w_ref, *rest, epilogue, tiles_per_seq, n_cast):
    rest = list(rest)
    extra = [rest.pop(0) for _ in range(PROJ_EXTRA_INPUTS[epilogue])]
    cast_in = [rest.pop(0) for _ in range(n_cast)]
    o_ref = rest.pop(0)
    cast_out = [rest.pop(0) for _ in range(n_cast)]
    w_sc = rest.pop(0)
    i = pl.program_id(1)

    @pl.when(i == 0)
    def _():
        w = w_ref[...].reshape(w_ref.shape[-2:]).astype(BF16)
        rows = w.shape[0]
        for r in range(w_sc.shape[0] // rows):
            w_sc[r * rows:(r + 1) * rows, :] = w

    for src, dst in zip(cast_in, cast_out):
        dst[...] = src[...].astype(BF16)

    acc = _dot_nt(u_ref[...], w_sc[...])
    if epilogue == "silu":
        acc = acc * jax.nn.sigmoid(acc)
    elif epilogue == "sigmoid_bias":
        acc = jax.nn.sigmoid(acc + extra[0][...])
    elif epilogue == "softplus_bias":
        acc = _softplus(acc + extra[0][...])
    elif epilogue == "conv_silu":
        convw_ref, convb_ref = extra
        carry_sc = rest.pop(0)
        tm = acc.shape[0]
        carry = jnp.where(i % tiles_per_seq == 0, 0.0, carry_sc[...])
        carry_sc[...] = acc[tm - CONV_CARRY:]
        ext = jnp.concatenate([carry, acc], axis=0)
        conv = convb_ref[...] + convw_ref[SSD_CONV - 1:SSD_CONV, :] * acc
        for k in range(SSD_CONV - 1):
            start = CONV_CARRY - (SSD_CONV - 1 - k)
            conv = conv + convw_ref[k:k + 1, :] * ext[start:start + tm]
        acc = conv * jax.nn.sigmoid(conv)
    o_ref[...] = acc.astype(o_ref.dtype)


def _proj(u, wt, col_start, n, epilogue, extra=(), to_cast=(), *, seq, layer, tm=1024, tn=1024, repeat=1,
          out_dtype=F32):
    m, k = u.shape
    tn = min(tn, n)
    assert n % tn == 0 and m % tm == 0 and seq % tm == 0 and (repeat == 1 or n == tn)
    steps_j, steps_i = n // tn, m // tm
    if col_start % tn == 0:
        col_block = col_start // tn
        w_spec = pl.BlockSpec((None, tn, k), lambda j, i: (layer, col_block + j, 0))
    else:
        assert col_start % F32_SUBLANES == 0 and tn % F32_SUBLANES == 0
        w_spec = pl.BlockSpec((pl.Element(1), pl.Element(tn), pl.Element(k)),
                              lambda j, i: (layer, pl.multiple_of(col_start + j * tn, F32_SUBLANES), 0))
    to = tn * repeat
    in_specs = [pl.BlockSpec((tm, k), lambda j, i: (i, 0)), w_spec]
    for e in extra:
        in_specs.append(pl.BlockSpec((e.shape[0], to), lambda j, i: (0, j)))
    out_specs = [pl.BlockSpec((tm, to), lambda j, i: (i, j))]
    out_shape = [jax.ShapeDtypeStruct((m, n * repeat), out_dtype)]
    for arr in to_cast:
        rows = arr.shape[0] // (steps_j * steps_i)
        assert rows * steps_j * steps_i == arr.shape[0] and rows % BF16_SUBLANES == 0, arr.shape
        spec = pl.BlockSpec((rows, arr.shape[1]), lambda j, i: (j * steps_i + i, 0))
        in_specs.append(spec)
        out_specs.append(spec)
        out_shape.append(jax.ShapeDtypeStruct(arr.shape, BF16))
    scratch = [pltpu.VMEM((to, k), BF16)]
    if epilogue == "conv_silu":
        scratch.append(pltpu.VMEM((CONV_CARRY, to), F32))
    outs = pl.pallas_call(
        functools.partial(_proj_kernel, epilogue=epilogue, tiles_per_seq=seq // tm, n_cast=len(to_cast)),
        grid=(steps_j, steps_i),
        in_specs=in_specs,
        out_specs=out_specs,
        out_shape=out_shape,
        scratch_shapes=scratch,
        compiler_params=_params(("parallel", "arbitrary")),
        name="proj_" + epilogue,
    )(u, wt, *extra, *to_cast)
    return outs if to_cast else outs[0]


def _pool_kernel(p_ref, halo_ref, ga_ref, pw_ref, scale_ref, wo_ref, o_ref, mixed_sc, *, tiles_per_seq):
    tm = p_ref.shape[0]
    i = pl.program_id(0)
    tile_in_seq = i % tiles_per_seq
    halo = jnp.where(tile_in_seq == 0, 0.0, halo_ref[...])
    pos = tile_in_seq * tm + lax.broadcasted_iota(jnp.int32, (tm, 1), 0)
    for g, window in enumerate(POOL_WINDOWS):
        cols = slice(g * POOL_GROUP_DIM, (g + 1) * POOL_GROUP_DIM)
        x = p_ref[:, cols]
        s = jnp.concatenate([halo[:, cols], x], axis=0)
        width = 1
        while width < window:
            s = s[width:] + s[:-width]
            width *= 2
        win = s[POOL_HALO - window + 1:]
        inv_cnt = 1.0 / jnp.minimum(pos + 1, window).astype(F32)
        pooled = (win * inv_cnt - x).astype(BF16)
        mixed = _dot(pooled, pw_ref[g]) * scale_ref[:, cols]
        mixed_sc[:, cols] = mixed.astype(BF16)
    o_ref[...] = (ga_ref[...].astype(F32) * _dot(mixed_sc[...], wo_ref[...])).astype(o_ref.dtype)


def _pool(p, gates, pool_w, pool_scale, w_out, seq, *, tm=512):
    m, d = p.shape
    halo_blocks = tm // POOL_HALO
    row = pl.BlockSpec((tm, d), lambda i: (i, 0))
    return pl.pallas_call(
        functools.partial(_pool_kernel, tiles_per_seq=seq // tm),
        grid=(m // tm,),
        in_specs=[row,
                  pl.BlockSpec((POOL_HALO, d), lambda i: (jnp.maximum(i * halo_blocks - 1, 0), 0)),
                  row,
                  _const_spec(pool_w.shape),
                  _const_spec((1, d)),
                  _const_spec(w_out.shape)],
        out_specs=row,
        out_shape=jax.ShapeDtypeStruct((m, d), BF16),
        scratch_shapes=[pltpu.VMEM((tm, d), BF16)],
        compiler_params=_params(("parallel",)),
        name="pool",
    )(p, p, gates, pool_w, pool_scale, w_out)


HEADS_PER_DOT = MXU_DIM // SSD_HEAD_DIM
SSD_CHUNKS_PER_STEP = 4


def _ssd_constants():
    lane = np.arange(SSD_INNER)
    head_of_lane = lane // SSD_HEAD_DIM
    pos_of_lane = lane % SSD_HEAD_DIM
    expand = np.zeros((MXU_DIM, SSD_INNER), np.float32)
    for piece in range(3):
        expand[piece * SSD_HEADS + head_of_lane, lane] = 1.0
    row = np.arange(CHUNK)[:, None]
    diag = (row == pos_of_lane[None, :]).astype(np.float32)
    causal = np.where(row >= pos_of_lane[None, :], 0.0, -np.inf).astype(np.float32)
    blk = np.arange(MXU_DIM) // SSD_HEAD_DIM
    blockdiag = (blk[:, None] == blk[None, :]).astype(np.float32)
    return (jnp.asarray(expand, BF16), jnp.asarray(diag), jnp.asarray(causal), jnp.asarray(blockdiag, BF16))


def _split_select(v, lower_half, pieces):
    p0 = v.astype(BF16).astype(F32)
    r1 = v - p0
    p1 = r1.astype(BF16).astype(F32)
    first = jnp.where(lower_half, p0, p1).astype(BF16)
    if pieces == 2:
        return first
    p2 = (r1 - p1).astype(BF16)
    return jnp.concatenate([first, p2], axis=1)


def _ssd_kernel(xbc_ref, dt_ref, alog_ref, dskip_ref, expand_ref, diag_ref, causal_ref, bd_ref, y_ref,
                state_sc, colb_sc, dtx_sc):
    @pl.when(pl.program_id(1) == 0)
    def _():
        state_sc[...] = jnp.zeros_like(state_sc)

    dt = dt_ref[...]
    a = dt * (-jnp.exp(alog_ref[...]) * LOG2_E)
    row_in_chunk = lax.broadcasted_iota(jnp.int32, a.shape, 0) % CHUNK
    lower_half = lax.broadcasted_iota(jnp.int32, a.shape, 1) < SSD_HEADS
    acs = a
    shift = 1
    while shift < CHUNK:
        acs = acs + jnp.where(row_in_chunk >= shift, pltpu.roll(acs, shift, axis=0), 0.0)
        shift *= 2

    colb_sc[...] = _dot(_split_select(acs, lower_half, 3), expand_ref[...])
    dtx_sc[...] = _dot(_split_select(dt, lower_half, 2), expand_ref[0:LANES, :])

    groups = range(SSD_GROUPS)
    group_cols = [slice(g * SSD_GROUP_WIDTH, (g + 1) * SSD_GROUP_WIDTH) for g in groups]
    parts = range(SSD_GROUP_WIDTH // MXU_DIM)
    for q in range(SSD_CHUNKS_PER_STEP):
        rows = slice(q * CHUNK, (q + 1) * CHUNK)
        b_g = [xbc_ref[rows, SSD_INNER + g * SSD_STATE:SSD_INNER + (g + 1) * SSD_STATE].astype(BF16)
               for g in groups]
        c_g = [xbc_ref[rows, SSD_INNER + SSD_BC + g * SSD_STATE:SSD_INNER + SSD_BC + (g + 1) * SSD_STATE]
               .astype(BF16) for g in groups]
        cb2 = [_dot_nt(c_g[g], jnp.concatenate([b_g[g], b_g[g]], axis=0)) for g in groups]
        y_prev = [_dot(c_g[g], state_sc[g].astype(BF16)) for g in groups]

        gmat, x_bd, xend_b, colb = [], [], [], []
        for g in groups:
            cols = group_cols[g]
            colb.append(colb_sc[rows, cols])
            rowb = jnp.sum(colb[g] * diag_ref[:, cols], axis=0, keepdims=True)
            decay_ls = jnp.exp2(colb[g] - rowb + causal_ref[:, cols])
            xdt = xbc_ref[rows, cols] * dtx_sc[rows, cols]
            xdt_b = xdt.astype(BF16)
            xend_b.append((xdt * jnp.exp2(colb[g][CHUNK - 1:CHUNK, :] - colb[g])).astype(BF16))
            cb = jnp.concatenate([cb2[g]] * (SSD_GROUP_WIDTH // LANES), axis=1)
            gmat.append((cb * decay_ls).astype(BF16))
            x_bd.append([jnp.concatenate([xdt_b[:, part * MXU_DIM:(part + 1) * MXU_DIM]] * HEADS_PER_DOT, axis=0)
                         * bd_ref[...] for part in parts])

        y_diag = [jnp.concatenate([_dot(gmat[g][:, part * MXU_DIM:(part + 1) * MXU_DIM], x_bd[g][part])
                                   for part in parts], axis=1) for g in groups]
        new_state = [lax.dot_general(b_g[g], xend_b[g], (((0,), (0,)), ((), ())), preferred_element_type=F32)
                     for g in groups]

        for g in groups:
            cols = group_cols[g]
            y_ref[rows, cols] = (y_prev[g] * jnp.exp2(colb[g]) + dskip_ref[:, cols] * xbc_ref[rows, cols]
                                 + y_diag[g]).astype(y_ref.dtype)
            state_sc[g] = state_sc[g] * jnp.exp2(colb[g][CHUNK - 1:CHUNK, :]) + new_state[g]


def _ssd(xbc, dt2, a_log2, dskip_x, batch):
    m = xbc.shape[0]
    rows = SSD_CHUNKS_PER_STEP * CHUNK
    steps = m // batch // rows
    assert steps * rows * batch == m
    expand, diag, causal, blockdiag = _ssd_constants()
    tok = lambda width: pl.BlockSpec((rows, width), lambda b, c: (b * steps + c, 0))
    return pl.pallas_call(
        _ssd_kernel,
        grid=(batch, steps),
        in_specs=[tok(SSD_CONV_DIM), tok(LANES), _const_spec(a_log2.shape),
                  _const_spec(dskip_x.shape), _const_spec(expand.shape), _const_spec(diag.shape),
                  _const_spec(causal.shape), _const_spec(blockdiag.shape)],
        out_specs=tok(SSD_INNER),
        out_shape=jax.ShapeDtypeStruct((m, SSD_INNER), BF16),
        scratch_shapes=[pltpu.VMEM((SSD_GROUPS, SSD_STATE, SSD_GROUP_WIDTH), F32),
                        pltpu.VMEM((rows, SSD_INNER), F32), pltpu.VMEM((rows, SSD_INNER), F32)],
        compiler_params=_params(("arbitrary", "arbitrary")),
        name="ssd",
    )(xbc, dt2, a_log2, dskip_x, expand, diag, causal, blockdiag)


def _merge_kernel(y_ref, z_ref, ng_ref, wso_ref, gaya_ref, gb_ref, wmix_ref, post_ref, h_ref, next_ref,
                  ho_ref, un_ref, yn_sc):
    yz = y_ref[...].astype(F32) * z_ref[...].astype(F32)
    for g in range(SSD_GROUPS):
        cols = slice(g * SSD_GROUP_WIDTH, (g + 1) * SSD_GROUP_WIDTH)
        yn_sc[:, cols] = _rms(yz[:, cols], ng_ref[:, cols]).astype(BF16)
    y_b = _dot(yn_sc[...], wso_ref[...])
    merged = (gaya_ref[...].astype(F32) + gb_ref[...].astype(F32) * y_b).astype(BF16)
    h = h_ref[...] + _rms(_dot(merged, wmix_ref[...]), post_ref[...])
    ho_ref[...] = h
    un_ref[...] = _rms(h, next_ref[...]).astype(BF16)


def _merge(y, z, norm_g, w_ssd_out, gaya, gates, w_mix, post_g, h, next_g, *, tm=256):
    m, d = h.shape
    wide = pl.BlockSpec((tm, SSD_INNER), lambda i: (i, 0))
    row = pl.BlockSpec((tm, d), lambda i: (i, 0))
    return pl.pallas_call(
        _merge_kernel,
        grid=(m // tm,),
        in_specs=[wide, wide, _const_spec((1, SSD_INNER)), _const_spec(w_ssd_out.shape),
                  row,
                  pl.BlockSpec((tm, d), lambda i: (i, 1)),
                  _const_spec(w_mix.shape), _const_spec((1, d)), row, _const_spec((1, d))],
        out_specs=[row, row],
        out_shape=[jax.ShapeDtypeStruct((m, d), F32), jax.ShapeDtypeStruct((m, d), BF16)],
        scratch_shapes=[pltpu.VMEM((tm, SSD_INNER), BF16)],
        compiler_params=_params(("parallel",)),
        name="merge",
    )(y, z, norm_g, w_ssd_out, gaya, gates, w_mix, post_g, h, next_g)


def _kv_kernel(mem_ref, g_ref, wk_ref, wv_ref, k_ref, v_ref):
    mem_n = _rms(mem_ref[...], g_ref[...]).astype(BF16)
    k_ref[...] = _dot(mem_n, wk_ref[...]).astype(BF16)
    v_ref[...] = _dot(mem_n, wv_ref[...]).astype(BF16)


def _kv(mem, g, w_k, w_v):
    rows = mem.shape[0]
    out = jax.ShapeDtypeStruct((rows, X_WIDTH), BF16)
    return pl.pallas_call(_kv_kernel, out_shape=[out, out], name="kv",
                          compiler_params=pltpu.CompilerParams(vmem_limit_bytes=V7X_VMEM_LIMIT))(mem, g, w_k, w_v)


def _attn_kernel(u_ref, h_ref, wq_ref, k_ref, v_ref, wo_ref, post_ref, ho_ref, o_sc):
    q = _dot(u_ref[...], wq_ref[...]).astype(BF16)
    heads = range(X_HEADS)
    head_cols = [slice(hd * X_HEAD_DIM, (hd + 1) * X_HEAD_DIM) for hd in heads]
    scores = [_dot_nt(q[:, cols], k_ref[:, cols]) * (1.0 / math.sqrt(X_HEAD_DIM)) for cols in head_cols]
    weights = [jnp.exp(s - jnp.max(s, axis=-1, keepdims=True)) for s in scores]
    denom = [jnp.sum(e, axis=-1, keepdims=True) for e in weights]
    mixed = [_dot(weights[hd].astype(BF16), v_ref[:, head_cols[hd]]) for hd in heads]
    for hd in heads:
        o_sc[:, head_cols[hd]] = (mixed[hd] / denom[hd]).astype(BF16)
    ho_ref[...] = h_ref[...] + _rms(_dot(o_sc[...], wo_ref[...]), post_ref[...])


def _attn(u, h, w_q, k, v, w_o, post_g, seq, *, tm=512):
    m, d = h.shape
    tiles_per_seq = seq // tm
    row = pl.BlockSpec((tm, d), lambda i: (i, 0))
    mem = pl.BlockSpec((N_MEM, X_WIDTH), lambda i: (i // tiles_per_seq, 0))
    return pl.pallas_call(
        _attn_kernel,
        grid=(m // tm,),
        in_specs=[row, row, _const_spec(w_q.shape), mem, mem, _const_spec(w_o.shape),
                  _const_spec((1, d))],
        out_specs=row,
        out_shape=jax.ShapeDtypeStruct((m, d), F32),
        scratch_shapes=[pltpu.VMEM((tm, X_WIDTH), BF16)],
        compiler_params=_params(("parallel",)),
        name="attn",
    )(u, h, w_q, k, v, w_o, post_g)


def kernel(x, mem, ffn1_pre_g, ffn1_w1, ffn1_w3, ffn1_w2, ffn1_post_g, mix_pre_g, w_in, b_gate, pool_w, pool_scale, w_pool_out, conv_w, conv_b, dt_bias, a_log, d_skip, ssd_norm_g, w_ssd_out, w_mix_out, mix_post_g, xattn_pre_g, mem_norm_g, w_q, w_k, w_v, w_xo, xattn_post_g, ffn2_pre_g, ffn2_w1, ffn2_w3, ffn2_w2, ffn2_post_g):
    batch, seq, d = x.shape
    depth = ffn1_w1.shape[0]
    c0 = D_MODEL
    c1 = c0 + SSD_INNER
    c2 = c1 + SSD_CONV_DIM
    c3 = c2 + SSD_HEADS
    vec = lambda g: g.reshape(1, -1).astype(F32)
    twice = lambda t: jnp.concatenate([t, t], axis=-1)

    h = x.reshape(batch * seq, d)
    mem2 = mem.reshape(batch * N_MEM, d)
    w_in_t = jnp.swapaxes(w_in, 1, 2)
    u = None
    for i in range(depth):
        bf = lambda w: w[i].astype(BF16)

        h, u = _ffn(h, vec(ffn1_pre_g[i]), bf(ffn1_w1), bf(ffn1_w3), bf(ffn1_w2), vec(ffn1_post_g[i]),
                    vec(mix_pre_g[i]))

        p_in, wb_q, wb_k, wb_v, wb_xo = _proj(u, w_in_t, 0, c0, "none", (), (w_q[i], w_k[i], w_v[i], w_xo[i]),
                                              seq=seq, layer=i)
        z_act, f2_w1, f2_w3, f2_w2 = _proj(u, w_in_t, c0, c1 - c0, "silu", (),
                                           (ffn2_w1[i], ffn2_w3[i], ffn2_w2[i]), seq=seq, layer=i,
                                           out_dtype=BF16)
        xbc = _proj(u, w_in_t, c1, c2 - c1, "conv_silu", (conv_w[i], vec(conv_b[i])), seq=seq, layer=i, tn=512)
        dt2 = _proj(u, w_in_t, c2, c3 - c2, "softplus_bias", (twice(vec(dt_bias[i])),), seq=seq, layer=i,
                    repeat=2)
        gates, wb_ssd_out, wb_mix_out, wb_pool_out, wb_pool = _proj(
            u, w_in_t, c3, 2 * D_MODEL, "sigmoid_bias", (vec(b_gate[i]),),
            (w_ssd_out[i], w_mix_out[i], w_pool_out[i], pool_w[i].reshape(-1, POOL_GROUP_DIM)),
            seq=seq, layer=i, out_dtype=BF16)

        gaya = _pool(p_in, gates, wb_pool.reshape(pool_w.shape[1:]), vec(pool_scale[i]), wb_pool_out, seq)
        y = _ssd(xbc, dt2, twice(vec(a_log[i])), jnp.repeat(vec(d_skip[i]), SSD_HEAD_DIM, axis=1), batch)
        h, u = _merge(y, z_act, vec(ssd_norm_g[i]), wb_ssd_out, gaya, gates, wb_mix_out,
                      vec(mix_post_g[i]), h, vec(xattn_pre_g[i]))

        k, v = _kv(mem2, vec(mem_norm_g[i]), wb_k, wb_v)
        h = _attn(u, h, wb_q, k, v, wb_xo, vec(xattn_post_g[i]), seq)

        last = i == depth - 1
        h, u = _ffn(h, vec(ffn2_pre_g[i]), f2_w1, f2_w3, f2_w2, vec(ffn2_post_g[i]),
                    None if last else vec(ffn1_pre_g[i + 1]))
    return h.reshape(batch, seq, d)
```

```python
import functools
import math

import jax
import jax.numpy as jnp
import numpy as np
from jax import lax
from jax.experimental import pallas as pl
from jax.experimental.pallas import tpu as pltpu

F32 = jnp.float32
BF16 = jnp.bfloat16

D_MODEL = 2048
D_FF = 5632
CHUNK = 64
N_MEM = 256
POOL_WINDOWS = (2, 4, 8, 16)
POOL_GROUP_DIM = D_MODEL // len(POOL_WINDOWS)
POOL_HALO = 16
SSD_INNER = 4096
SSD_HEAD_DIM = 64
SSD_HEADS = SSD_INNER // SSD_HEAD_DIM
SSD_GROUPS = 8
SSD_STATE = 128
SSD_GROUP_WIDTH = SSD_INNER // SSD_GROUPS
SSD_BC = SSD_GROUPS * SSD_STATE
SSD_CONV = 4
SSD_CONV_DIM = SSD_INNER + 2 * SSD_BC
X_HEADS = 4
X_HEAD_DIM = 128
X_WIDTH = X_HEADS * X_HEAD_DIM
EPS = 1e-6
LOG2_E = math.log2(math.e)

LANES = 128
F32_SUBLANES = 8
BF16_SUBLANES = 16
MXU_DIM = 256
V7X_VMEM_LIMIT = 60 * 1024 * 1024


def _params(semantics, flags=None):
    return pltpu.CompilerParams(dimension_semantics=semantics, vmem_limit_bytes=V7X_VMEM_LIMIT, flags=flags)


def _rms(x, g):
    return x * lax.rsqrt(jnp.mean(x * x, axis=-1, keepdims=True) + EPS) * g


def _dot(a, b):
    return jnp.dot(a, b, preferred_element_type=F32)


def _dot_nt(a, b):
    return lax.dot_general(a, b, (((1,), (1,)), ((), ())), preferred_element_type=F32)


def _const_spec(shape):
    return pl.BlockSpec(shape, lambda *_: (0,) * len(shape), pipeline_mode=pl.Buffered(1))


def _ffn_kernel(x_ref, pre_ref, w1_ref, w3_ref, w2_ref, post_ref, *rest, emit_next, emit_weights, has_head):
    rest = list(rest)
    next_ref = rest.pop(0) if emit_next else None
    head_refs = [rest.pop(0) for _ in range(2 if emit_next else 1)] if has_head else None
    h_ref = rest.pop(0)
    un_ref = rest.pop(0) if emit_next else None
    wb_refs = [rest.pop(0) for _ in range(3)] if emit_weights else None
    u_sc, acc_sc = rest
    i, j = pl.program_id(0), pl.program_id(1)
    last = pl.num_programs(1) - 1

    def body():
        @pl.when(j == 0)
        def _():
            u_sc[...] = _rms(x_ref[...], pre_ref[...]).astype(BF16)
            acc_sc[...] = jnp.zeros_like(acc_sc)

        w1, w3, w2 = w1_ref[...], w3_ref[...], w2_ref[...]
        if emit_weights:
            w1, w3, w2 = w1.astype(BF16), w3.astype(BF16), w2.astype(BF16)
            for dst, w in zip(wb_refs, (w1, w3, w2)):
                dst[...] = w
        u = u_sc[...]
        a = _dot(u, w1)
        b = _dot(u, w3)
        act = (a * jax.nn.sigmoid(a) * b).astype(BF16)
        acc_sc[...] += _dot(act, w2)

        @pl.when(j == last)
        def _():
            h = x_ref[...] + 0.5 * _rms(acc_sc[...], post_ref[...])
            h_ref[...] = h
            if emit_next:
                un_ref[...] = _rms(h, next_ref[...]).astype(BF16)

    if has_head:
        @pl.when(jnp.logical_and(i == 0, j == last))
        def _():
            h_ref[...] = head_refs[0][...]
            if emit_next:
                un_ref[...] = head_refs[1][...]

        pl.when(i > 0)(body)
    else:
        body()


FFN_TM = 512
FFN_TF = 512
FFN_HEAD_TF = 256


def _ffn(x, pre_g, w1, w3, w2, post_g, next_g, *, tm=FFN_TM, tf=FFN_TF, rows=None, head=None):
    m, d = x.shape
    rows = m if rows is None else rows
    dff = w1.shape[1]
    emit_next = next_g is not None
    emit_weights = w1.dtype != BF16
    has_head = head is not None
    jw = (lambda i, j: jnp.where(i == 0, 0, j)) if has_head else (lambda i, j: j)
    row = pl.BlockSpec((tm, d), lambda i, j: (i, 0))
    vec = pl.BlockSpec((1, d), lambda i, j: (0, 0))
    w_up = pl.BlockSpec((d, tf), lambda i, j: (0, jw(i, j)))
    w_down = pl.BlockSpec((tf, d), lambda i, j: (jw(i, j), 0))
    in_specs = [row, vec, w_up, w_up, w_down, vec]
    args = [x, pre_g, w1, w3, w2, post_g]
    out_shape = [jax.ShapeDtypeStruct((rows, d), F32)]
    out_specs = [row]
    if emit_next:
        in_specs.append(vec)
        args.append(next_g)
        out_shape.append(jax.ShapeDtypeStruct((rows, d), BF16))
        out_specs.append(row)
    if has_head:
        for t in head[:2 if emit_next else 1]:
            in_specs.append(_const_spec(t.shape))
            args.append(t)
    if emit_weights:
        out_shape += [jax.ShapeDtypeStruct(w.shape, BF16) for w in (w1, w3, w2)]
        out_specs += [w_up, w_up, w_down]
    outs = pl.pallas_call(
        functools.partial(_ffn_kernel, emit_next=emit_next, emit_weights=emit_weights, has_head=has_head),
        grid=(rows // tm, dff // tf),
        in_specs=in_specs,
        out_specs=out_specs,
        out_shape=out_shape,
        scratch_shapes=[pltpu.VMEM((tm, d), BF16), pltpu.VMEM((tm, d), F32)],
        compiler_params=_params(("parallel", "arbitrary")),
        name="ffn",
    )(*args)
    n_main = 2 if emit_next else 1
    return outs[0], (outs[1] if emit_next else None), (tuple(outs[n_main:]) if emit_weights else None)


def _softplus(x):
    return jnp.maximum(x, 0.0) + jnp.log1p(jnp.exp(-jnp.abs(x)))


CONV_CARRY = 8
PROJ_EXTRA_INPUTS = {"none": 0, "silu": 0, "sigmoid_bias": 1, "softplus_bias": 1, "conv_silu": 2}


def _proj_kernel(u_ref, w_ref, *rest, epilogue, tiles_per_seq, n_cast):
    rest = list(rest)
    extra = [rest.pop(0) for _ in range(PROJ_EXTRA_INPUTS[epilogue])]
    cast_in = [rest.pop(0) for _ in range(n_cast)]
    o_ref = rest.pop(0)
    cast_out = [rest.pop(0) for _ in range(n_cast)]
    w_sc = rest.pop(0)
    i = pl.program_id(1)

    @pl.when(i == 0)
    def _():
        w = w_ref[...].reshape(w_ref.shape[-2:]).astype(BF16)
        rows = w.shape[0]
        for r in range(w_sc.shape[0] // rows):
            w_sc[r * rows:(r + 1) * rows, :] = w

    for src, dst in zip(cast_in, cast_out):
        dst[...] = src[...].astype(BF16)

    acc = _dot_nt(u_ref[...], w_sc[...])
    if epilogue == "silu":
        acc = acc * jax.nn.sigmoid(acc)
    elif epilogue == "sigmoid_bias":
        acc = jax.nn.sigmoid(acc + extra[0][...])
    elif epilogue == "softplus_bias":
        acc = _softplus(acc + extra[0][...])
    elif epilogue == "conv_silu":
        convw_ref, convb_ref = extra
        carry_sc = rest.pop(0)
        tm = acc.shape[0]
        carry = jnp.where(i % tiles_per_seq == 0, 0.0, carry_sc[...])
        carry_sc[...] = acc[tm - CONV_CARRY:]
        ext = jnp.concatenate([carry, acc], axis=0)
        conv = convb_ref[...] + convw_ref[SSD_CONV - 1:SSD_CONV, :] * acc
        for k in range(SSD_CONV - 1):
            start = CONV_CARRY - (SSD_CONV - 1 - k)
            conv = conv + convw_ref[k:k + 1, :] * ext[start:start + tm]
        acc = conv * jax.nn.sigmoid(conv)
    o_ref[...] = acc.astype(o_ref.dtype)


def _proj(u, wt, col_start, n, epilogue, extra=(), to_cast=(), *, seq, layer, tm=1024, tn=1024, repeat=1,
          out_dtype=F32):
    m, k = u.shape
    tn = min(tn, n)
    assert n % tn == 0 and m % tm == 0 and seq % tm == 0 and (repeat == 1 or n == tn)
    steps_j, steps_i = n // tn, m // tm
    if col_start % tn == 0:
        col_block = col_start // tn
        w_spec = pl.BlockSpec((None, tn, k), lambda j, i: (layer, col_block + j, 0))
    else:
        assert col_start % F32_SUBLANES == 0 and tn % F32_SUBLANES == 0
        w_spec = pl.BlockSpec((pl.Element(1), pl.Element(tn), pl.Element(k)),
                              lambda j, i: (layer, pl.multiple_of(col_start + j * tn, F32_SUBLANES), 0))
    to = tn * repeat
    in_specs = [pl.BlockSpec((tm, k), lambda j, i: (i, 0)), w_spec]
    for e in extra:
        in_specs.append(pl.BlockSpec((e.shape[0], to), lambda j, i: (0, j)))
    out_specs = [pl.BlockSpec((tm, to), lambda j, i: (i, j))]
    out_shape = [jax.ShapeDtypeStruct((m, n * repeat), out_dtype)]
    for arr in to_cast:
        rows = arr.shape[0] // (steps_j * steps_i)
        assert rows * steps_j * steps_i == arr.shape[0] and rows % BF16_SUBLANES == 0, arr.shape
        spec = pl.BlockSpec((rows, arr.shape[1]), lambda j, i: (j * steps_i + i, 0))
        in_specs.append(spec)
        out_specs.append(spec)
        out_shape.append(jax.ShapeDtypeStruct(arr.shape, BF16))
    scratch = [pltpu.VMEM((to, k), BF16)]
    if epilogue == "conv_silu":
        scratch.append(pltpu.VMEM((CONV_CARRY, to), F32))
    outs = pl.pallas_call(
        functools.partial(_proj_kernel, epilogue=epilogue, tiles_per_seq=seq // tm, n_cast=len(to_cast)),
        grid=(steps_j, steps_i),
        in_specs=in_specs,
        out_specs=out_specs,
        out_shape=out_shape,
        scratch_shapes=scratch,
        compiler_params=_params(("parallel", "arbitrary")),
        name="proj_" + epilogue,
    )(u, wt, *extra, *to_cast)
    return outs if to_cast else outs[0]


def _pool_kernel(p_ref, halo_ref, ga_ref, pw_ref, scale_ref, wo_ref, o_ref, mixed_sc, *, tiles_per_seq):
    tm = p_ref.shape[0]
    i = pl.program_id(0)
    tile_in_seq = i % tiles_per_seq
    halo = jnp.where(tile_in_seq == 0, 0.0, halo_ref[...])
    pos = tile_in_seq * tm + lax.broadcasted_iota(jnp.int32, (tm, 1), 0)
    group_cols = [slice(g * POOL_GROUP_DIM, (g + 1) * POOL_GROUP_DIM) for g in range(len(POOL_WINDOWS))]
    pooled = []
    for cols, window in zip(group_cols, POOL_WINDOWS):
        x = p_ref[:, cols]
        s = jnp.concatenate([halo[:, cols], x], axis=0)
        width = 1
        while width < window:
            s = s[width:] + s[:-width]
            width *= 2
        win = s[POOL_HALO - window + 1:]
        inv_cnt = 1.0 / jnp.minimum(pos + 1, window).astype(F32)
        pooled.append((win * inv_cnt - x).astype(BF16))
    mixed = [_dot(pooled[g], pw_ref[g]) for g in range(len(POOL_WINDOWS))]
    for g, cols in enumerate(group_cols):
        mixed_sc[:, cols] = (mixed[g] * scale_ref[:, cols]).astype(BF16)
    o_ref[...] = (ga_ref[...].astype(F32) * _dot(mixed_sc[...], wo_ref[...])).astype(o_ref.dtype)


def _pool(p, gates, pool_w, pool_scale, w_out, seq, *, tm=512):
    m, d = p.shape
    halo_blocks = tm // POOL_HALO
    row = pl.BlockSpec((tm, d), lambda i: (i, 0))
    return pl.pallas_call(
        functools.partial(_pool_kernel, tiles_per_seq=seq // tm),
        grid=(m // tm,),
        in_specs=[row,
                  pl.BlockSpec((POOL_HALO, d), lambda i: (jnp.maximum(i * halo_blocks - 1, 0), 0)),
                  row,
                  _const_spec(pool_w.shape),
                  _const_spec((1, d)),
                  _const_spec(w_out.shape)],
        out_specs=row,
        out_shape=jax.ShapeDtypeStruct((m, d), BF16),
        scratch_shapes=[pltpu.VMEM((tm, d), BF16)],
        compiler_params=_params(("parallel",)),
        name="pool",
    )(p, p, gates, pool_w, pool_scale, w_out)


HEADS_PER_DOT = MXU_DIM // SSD_HEAD_DIM
SSD_CHUNKS_PER_STEP = 4


def _ssd_constants():
    lane = np.arange(SSD_INNER)
    head_of_lane = lane // SSD_HEAD_DIM
    pos_of_lane = lane % SSD_HEAD_DIM
    expand = np.zeros((MXU_DIM, SSD_INNER), np.float32)
    for piece in range(3):
        expand[piece * SSD_HEADS + head_of_lane, lane] = 1.0
    row = np.arange(CHUNK)[:, None]
    diag = (row == pos_of_lane[None, :]).astype(np.float32)
    causal = np.where(row >= pos_of_lane[None, :], 0.0, -np.inf).astype(np.float32)
    blk = np.arange(MXU_DIM) // SSD_HEAD_DIM
    blockdiag = (blk[:, None] == blk[None, :]).astype(np.float32)
    return (jnp.asarray(expand, BF16), jnp.asarray(diag), jnp.asarray(causal), jnp.asarray(blockdiag, BF16))


def _split_select(v, lower_half, pieces):
    p0 = v.astype(BF16).astype(F32)
    r1 = v - p0
    p1 = r1.astype(BF16).astype(F32)
    first = jnp.where(lower_half, p0, p1).astype(BF16)
    if pieces == 2:
        return first
    p2 = (r1 - p1).astype(BF16)
    return jnp.concatenate([first, p2], axis=1)


def _ssd_kernel(xbc_ref, dt_ref, alog_ref, dskip_ref, expand_ref, diag_ref, causal_ref, bd_ref, y_ref,
                state_sc, colb_sc, dtx_sc):
    @pl.when(pl.program_id(1) == 0)
    def _():
        state_sc[...] = jnp.zeros_like(state_sc)

    dt = dt_ref[...]
    a = dt * (-jnp.exp(alog_ref[...]) * LOG2_E)
    row_in_chunk = lax.broadcasted_iota(jnp.int32, a.shape, 0) % CHUNK
    lower_half = lax.broadcasted_iota(jnp.int32, a.shape, 1) < SSD_HEADS
    acs = a
    shift = 1
    while shift < CHUNK:
        acs = acs + jnp.where(row_in_chunk >= shift, pltpu.roll(acs, shift, axis=0), 0.0)
        shift *= 2

    colb_sc[...] = _dot(_split_select(acs, lower_half, 3), expand_ref[...])
    dtx_sc[...] = _dot(_split_select(dt, lower_half, 2), expand_ref[0:LANES, :])

    groups = range(SSD_GROUPS)
    group_cols = [slice(g * SSD_GROUP_WIDTH, (g + 1) * SSD_GROUP_WIDTH) for g in groups]
    parts = range(SSD_GROUP_WIDTH // MXU_DIM)
    for q in range(SSD_CHUNKS_PER_STEP):
        rows = slice(q * CHUNK, (q + 1) * CHUNK)
        b_g = [xbc_ref[rows, SSD_INNER + g * SSD_STATE:SSD_INNER + (g + 1) * SSD_STATE].astype(BF16)
               for g in groups]
        c_g = [xbc_ref[rows, SSD_INNER + SSD_BC + g * SSD_STATE:SSD_INNER + SSD_BC + (g + 1) * SSD_STATE]
               .astype(BF16) for g in groups]
        cb2 = [_dot_nt(c_g[g], jnp.concatenate([b_g[g], b_g[g]], axis=0)) for g in groups]
        y_prev = [_dot(c_g[g], state_sc[g].astype(BF16)) for g in groups]

        gmat, x_bd, xend_b, colb = [], [], [], []
        for g in groups:
            cols = group_cols[g]
            colb.append(colb_sc[rows, cols])
            rowb = jnp.sum(colb[g] * diag_ref[:, cols], axis=0, keepdims=True)
            decay_ls = jnp.exp2(colb[g] - rowb + causal_ref[:, cols])
            xdt = xbc_ref[rows, cols] * dtx_sc[rows, cols]
            xdt_b = xdt.astype(BF16)
            xend_b.append((xdt * jnp.exp2(colb[g][CHUNK - 1:CHUNK, :] - colb[g])).astype(BF16))
            cb = jnp.concatenate([cb2[g]] * (SSD_GROUP_WIDTH // LANES), axis=1)
            gmat.append((cb * decay_ls).astype(BF16))
            x_bd.append([jnp.concatenate([xdt_b[:, part * MXU_DIM:(part + 1) * MXU_DIM]] * HEADS_PER_DOT, axis=0)
                         * bd_ref[...] for part in parts])

        y_diag = [jnp.concatenate([_dot(gmat[g][:, part * MXU_DIM:(part + 1) * MXU_DIM], x_bd[g][part])
                                   for part in parts], axis=1) for g in groups]
        new_state = [lax.dot_general(b_g[g], xend_b[g], (((0,), (0,)), ((), ())), preferred_element_type=F32)
                     for g in groups]

        for g in groups:
            cols = group_cols[g]
            y_ref[rows, cols] = (y_prev[g] * jnp.exp2(colb[g]) + dskip_ref[:, cols] * xbc_ref[rows, cols]
                                 + y_diag[g]).astype(y_ref.dtype)
            state_sc[g] = state_sc[g] * jnp.exp2(colb[g][CHUNK - 1:CHUNK, :]) + new_state[g]


def _ssd(xbc, dt2, a_log2, dskip_x, batch):
    m = xbc.shape[0]
    rows = SSD_CHUNKS_PER_STEP * CHUNK
    steps = m // batch // rows
    assert steps * rows * batch == m
    expand, diag, causal, blockdiag = _ssd_constants()
    tok = lambda width: pl.BlockSpec((rows, width), lambda b, c: (b * steps + c, 0))
    return pl.pallas_call(
        _ssd_kernel,
        grid=(batch, steps),
        in_specs=[tok(SSD_CONV_DIM), tok(LANES), _const_spec(a_log2.shape),
                  _const_spec(dskip_x.shape), _const_spec(expand.shape), _const_spec(diag.shape),
                  _const_spec(causal.shape), _const_spec(blockdiag.shape)],
        out_specs=tok(SSD_INNER),
        out_shape=jax.ShapeDtypeStruct((m, SSD_INNER), BF16),
        scratch_shapes=[pltpu.VMEM((SSD_GROUPS, SSD_STATE, SSD_GROUP_WIDTH), F32),
                        pltpu.VMEM((rows, SSD_INNER), F32), pltpu.VMEM((rows, SSD_INNER), F32)],
        compiler_params=_params(("arbitrary", "arbitrary")),
        name="ssd",
    )(xbc, dt2, a_log2, dskip_x, expand, diag, causal, blockdiag)


def _merge_kernel(y_ref, z_ref, ng_ref, wso_ref, gaya_ref, gb_ref, wmix_ref, post_ref, h_ref, next_ref,
                  ho_ref, un_ref, yn_sc):
    yz = y_ref[...].astype(F32) * z_ref[...].astype(F32)
    for g in range(SSD_GROUPS):
        cols = slice(g * SSD_GROUP_WIDTH, (g + 1) * SSD_GROUP_WIDTH)
        yn_sc[:, cols] = _rms(yz[:, cols], ng_ref[:, cols]).astype(BF16)
    y_b = _dot(yn_sc[...], wso_ref[...])
    merged = (gaya_ref[...].astype(F32) + gb_ref[...].astype(F32) * y_b).astype(BF16)
    h = h_ref[...] + _rms(_dot(merged, wmix_ref[...]), post_ref[...])
    ho_ref[...] = h
    un_ref[...] = _rms(h, next_ref[...]).astype(BF16)


def _merge(y, z, norm_g, w_ssd_out, gaya, gates, w_mix, post_g, h, next_g, *, tm=256):
    m, d = h.shape
    wide = pl.BlockSpec((tm, SSD_INNER), lambda i: (i, 0))
    row = pl.BlockSpec((tm, d), lambda i: (i, 0))
    return pl.pallas_call(
        _merge_kernel,
        grid=(m // tm,),
        in_specs=[wide, wide, _const_spec((1, SSD_INNER)), _const_spec(w_ssd_out.shape),
                  row,
                  pl.BlockSpec((tm, d), lambda i: (i, 1)),
                  _const_spec(w_mix.shape), _const_spec((1, d)), row, _const_spec((1, d))],
        out_specs=[row, row],
        out_shape=[jax.ShapeDtypeStruct((m, d), F32), jax.ShapeDtypeStruct((m, d), BF16)],
        scratch_shapes=[pltpu.VMEM((tm, SSD_INNER), BF16)],
        compiler_params=_params(("parallel",)),
        name="merge",
    )(y, z, norm_g, w_ssd_out, gaya, gates, w_mix, post_g, h, next_g)


def _kv_kernel(mem_ref, g_ref, wk_ref, wv_ref, k_ref, v_ref):
    mem_n = _rms(mem_ref[...], g_ref[...]).astype(BF16)
    k_ref[...] = _dot(mem_n, wk_ref[...]).astype(BF16)
    v_ref[...] = _dot(mem_n, wv_ref[...]).astype(BF16)


def _kv(mem, g, w_k, w_v):
    rows = mem.shape[0]
    out = jax.ShapeDtypeStruct((rows, X_WIDTH), BF16)
    return pl.pallas_call(_kv_kernel, out_shape=[out, out], name="kv",
                          compiler_params=pltpu.CompilerParams(vmem_limit_bytes=V7X_VMEM_LIMIT))(mem, g, w_k, w_v)


def _attn_kernel(u_ref, h_ref, wq_ref, k_ref, v_ref, wo_ref, post_ref, ho_ref, o_sc):
    q = _dot(u_ref[...], wq_ref[...]).astype(BF16)
    heads = range(X_HEADS)
    head_cols = [slice(hd * X_HEAD_DIM, (hd + 1) * X_HEAD_DIM) for hd in heads]
    scores = [_dot_nt(q[:, cols], k_ref[:, cols]) * (1.0 / math.sqrt(X_HEAD_DIM)) for cols in head_cols]
    weights = [jnp.exp(s - jnp.max(s, axis=-1, keepdims=True)) for s in scores]
    denom = [jnp.sum(e, axis=-1, keepdims=True) for e in weights]
    mixed = [_dot(weights[hd].astype(BF16), v_ref[:, head_cols[hd]]) for hd in heads]
    for hd in heads:
        o_sc[:, head_cols[hd]] = (mixed[hd] / denom[hd]).astype(BF16)
    ho_ref[...] = h_ref[...] + _rms(_dot(o_sc[...], wo_ref[...]), post_ref[...])


def _attn(u, h, w_q, k, v, w_o, post_g, seq, *, tm=512):
    m, d = h.shape
    tiles_per_seq = seq // tm
    row = pl.BlockSpec((tm, d), lambda i: (i, 0))
    mem = pl.BlockSpec((N_MEM, X_WIDTH), lambda i: (i // tiles_per_seq, 0))
    return pl.pallas_call(
        _attn_kernel,
        grid=(m // tm,),
        in_specs=[row, row, _const_spec(w_q.shape), mem, mem, _const_spec(w_o.shape),
                  _const_spec((1, d))],
        out_specs=row,
        out_shape=jax.ShapeDtypeStruct((m, d), F32),
        scratch_shapes=[pltpu.VMEM((tm, X_WIDTH), BF16)],
        compiler_params=_params(("parallel",)),
        name="attn",
    )(u, h, w_q, k, v, w_o, post_g)


def kernel(x, mem, ffn1_pre_g, ffn1_w1, ffn1_w3, ffn1_w2, ffn1_post_g, mix_pre_g, w_in, b_gate, pool_w, pool_scale, w_pool_out, conv_w, conv_b, dt_bias, a_log, d_skip, ssd_norm_g, w_ssd_out, w_mix_out, mix_post_g, xattn_pre_g, mem_norm_g, w_q, w_k, w_v, w_xo, xattn_post_g, ffn2_pre_g, ffn2_w1, ffn2_w3, ffn2_w2, ffn2_post_g):
    batch, seq, d = x.shape
    depth = ffn1_w1.shape[0]
    c0 = D_MODEL
    c1 = c0 + SSD_INNER
    c2 = c1 + SSD_CONV_DIM
    c3 = c2 + SSD_HEADS
    vec = lambda g: g.reshape(1, -1).astype(F32)
    twice = lambda t: jnp.concatenate([t, t], axis=-1)

    h = x.reshape(batch * seq, d)
    mem2 = mem.reshape(batch * N_MEM, d)
    w_in_t = jnp.swapaxes(w_in, 1, 2)
    u = None
    for i in range(depth):
        ffn1 = (vec(ffn1_pre_g[i]), ffn1_w1[i], ffn1_w3[i], ffn1_w2[i], vec(ffn1_post_g[i]), vec(mix_pre_g[i]))
        h_head, u_head, ffn1_wb = _ffn(h, *ffn1, tf=FFN_HEAD_TF, rows=FFN_TM)
        h, u, _ = _ffn(h, ffn1[0], *ffn1_wb, *ffn1[4:], head=(h_head, u_head))

        p_in, wb_q, wb_k, wb_v, wb_xo = _proj(u, w_in_t, 0, c0, "none", (), (w_q[i], w_k[i], w_v[i], w_xo[i]),
                                              seq=seq, layer=i)
        z_act, f2_w1, f2_w3, f2_w2 = _proj(u, w_in_t, c0, c1 - c0, "silu", (),
                                           (ffn2_w1[i], ffn2_w3[i], ffn2_w2[i]), seq=seq, layer=i,
                                           out_dtype=BF16)
        xbc = _proj(u, w_in_t, c1, c2 - c1, "conv_silu", (conv_w[i], vec(conv_b[i])), seq=seq, layer=i)
        dt2 = _proj(u, w_in_t, c2, c3 - c2, "softplus_bias", (twice(vec(dt_bias[i])),), seq=seq, layer=i,
                    repeat=2)
        gates, wb_ssd_out, wb_mix_out, wb_pool_out, wb_pool = _proj(
            u, w_in_t, c3, 2 * D_MODEL, "sigmoid_bias", (vec(b_gate[i]),),
            (w_ssd_out[i], w_mix_out[i], w_pool_out[i], pool_w[i].reshape(-1, POOL_GROUP_DIM)),
            seq=seq, layer=i, out_dtype=BF16)

        gaya = _pool(p_in, gates, wb_pool.reshape(pool_w.shape[1:]), vec(pool_scale[i]), wb_pool_out, seq)
        y = _ssd(xbc, dt2, twice(vec(a_log[i])), jnp.repeat(vec(d_skip[i]), SSD_HEAD_DIM, axis=1), batch)
        h, u = _merge(y, z_act, vec(ssd_norm_g[i]), wb_ssd_out, gaya, gates, wb_mix_out,
                      vec(mix_post_g[i]), h, vec(xattn_pre_g[i]))

        k, v = _kv(mem2, vec(mem_norm_g[i]), wb_k, wb_v)
        h = _attn(u, h, wb_q, k, v, wb_xo, vec(xattn_post_g[i]), seq)

        last = i == depth - 1
        h, u, _ = _ffn(h, vec(ffn2_pre_g[i]), f2_w1, f2_w3, f2_w2, vec(ffn2_post_g[i]),
                       None if last else vec(ffn1_pre_g[i + 1]))
    return h.reshape(batch, seq, d)
```

```python
import functools
import math

import jax
import jax.numpy as jnp
import numpy as np
from jax import lax
from jax.experimental import pallas as pl
from jax.experimental.pallas import tpu as pltpu

F32 = jnp.float32
BF16 = jnp.bfloat16

D_MODEL = 2048
D_FF = 5632
CHUNK = 64
N_MEM = 256
POOL_WINDOWS = (2, 4, 8, 16)
POOL_GROUP_DIM = D_MODEL // len(POOL_WINDOWS)
POOL_HALO = 16
SSD_INNER = 4096
SSD_HEAD_DIM = 64
SSD_HEADS = SSD_INNER // SSD_HEAD_DIM
SSD_GROUPS = 8
SSD_STATE = 128
SSD_GROUP_WIDTH = SSD_INNER // SSD_GROUPS
SSD_BC = SSD_GROUPS * SSD_STATE
SSD_CONV = 4
SSD_CONV_DIM = SSD_INNER + 2 * SSD_BC
X_HEADS = 4
X_HEAD_DIM = 128
X_WIDTH = X_HEADS * X_HEAD_DIM
EPS = 1e-6
LOG2_E = math.log2(math.e)

LANES = 128
F32_SUBLANES = 8
BF16_SUBLANES = 16
MXU_DIM = 256
V7X_VMEM_LIMIT = 60 * 1024 * 1024


def _params(semantics):
    return pltpu.CompilerParams(dimension_semantics=semantics, vmem_limit_bytes=V7X_VMEM_LIMIT)


def _rms(x, g):
    return x * lax.rsqrt(jnp.mean(x * x, axis=-1, keepdims=True) + EPS) * g


def _dot(a, b):
    return jnp.dot(a, b, preferred_element_type=F32)


def _dot_nt(a, b):
    return lax.dot_general(a, b, (((1,), (1,)), ((), ())), preferred_element_type=F32)


def _const_spec(shape):
    return pl.BlockSpec(shape, lambda *_: (0,) * len(shape), pipeline_mode=pl.Buffered(1))


def _ffn_kernel(x_ref, pre_ref, w1_ref, w3_ref, w2_ref, post_ref, *rest, emit_next, emit_weights, has_head):
    rest = list(rest)
    next_ref = rest.pop(0) if emit_next else None
    head_refs = [rest.pop(0) for _ in range(2 if emit_next else 1)] if has_head else None
    h_ref = rest.pop(0)
    un_ref = rest.pop(0) if emit_next else None
    wb_refs = [rest.pop(0) for _ in range(3)] if emit_weights else None
    u_sc, acc_sc = rest
    i, j = pl.program_id(0), pl.program_id(1)
    last = pl.num_programs(1) - 1

    def body():
        @pl.when(j == 0)
        def _():
            u_sc[...] = _rms(x_ref[...], pre_ref[...]).astype(BF16)
            acc_sc[...] = jnp.zeros_like(acc_sc)

        w1, w3, w2 = w1_ref[...], w3_ref[...], w2_ref[...]
        if emit_weights:
            w1, w3, w2 = w1.astype(BF16), w3.astype(BF16), w2.astype(BF16)
            for dst, w in zip(wb_refs, (w1, w3, w2)):
                dst[...] = w
        u = u_sc[...]
        a = _dot(u, w1)
        b = _dot(u, w3)
        act = (a * jax.nn.sigmoid(a) * b).astype(BF16)
        acc_sc[...] += _dot(act, w2)

        @pl.when(j == last)
        def _():
            h = x_ref[...] + _rms(acc_sc[...], post_ref[...])
            h_ref[...] = h
            if emit_next:
                un_ref[...] = _rms(h, next_ref[...]).astype(BF16)

    if has_head:
        @pl.when(jnp.logical_and(i == 0, j == last))
        def _():
            h_ref[...] = head_refs[0][...]
            if emit_next:
                un_ref[...] = head_refs[1][...]

        pl.when(i > 0)(body)
    else:
        body()


FFN_TM = 512
FFN_TF = 512
FFN_HEAD_TF = 256


def _ffn(x, pre_g, w1, w3, w2, post_g, next_g, *, tm=FFN_TM, tf=FFN_TF, rows=None, head=None):
    m, d = x.shape
    rows = m if rows is None else rows
    dff = w1.shape[1]
    emit_next = next_g is not None
    emit_weights = w1.dtype != BF16
    has_head = head is not None
    jw = (lambda i, j: jnp.where(i == 0, 0, j)) if has_head else (lambda i, j: j)
    row = pl.BlockSpec((tm, d), lambda i, j: (i, 0))
    vec = pl.BlockSpec((1, d), lambda i, j: (0, 0))
    w_up = pl.BlockSpec((d, tf), lambda i, j: (0, jw(i, j)))
    w_down = pl.BlockSpec((tf, d), lambda i, j: (jw(i, j), 0))
    in_specs = [row, vec, w_up, w_up, w_down, vec]
    args = [x, pre_g, w1, w3, w2, post_g]
    out_shape = [jax.ShapeDtypeStruct((rows, d), F32)]
    out_specs = [row]
    if emit_next:
        in_specs.append(vec)
        args.append(next_g)
        out_shape.append(jax.ShapeDtypeStruct((rows, d), BF16))
        out_specs.append(row)
    if has_head:
        for t in head[:2 if emit_next else 1]:
            in_specs.append(_const_spec(t.shape))
            args.append(t)
    if emit_weights:
        out_shape += [jax.ShapeDtypeStruct(w.shape, BF16) for w in (w1, w3, w2)]
        out_specs += [w_up, w_up, w_down]
    outs = pl.pallas_call(
        functools.partial(_ffn_kernel, emit_next=emit_next, emit_weights=emit_weights, has_head=has_head),
        grid=(rows // tm, dff // tf),
        in_specs=in_specs,
        out_specs=out_specs,
        out_shape=out_shape,
        scratch_shapes=[pltpu.VMEM((tm, d), BF16), pltpu.VMEM((tm, d), F32)],
        compiler_params=_params(("parallel", "arbitrary")),
        name="ffn",
    )(*args)
    n_main = 2 if emit_next else 1
    return outs[0], (outs[1] if emit_next else None), (tuple(outs[n_main:]) if emit_weights else None)


def _softplus(x):
    return jnp.maximum(x, 0.0) + jnp.log1p(jnp.exp(-jnp.abs(x)))


CONV_CARRY = 8
PROJ_EXTRA_INPUTS = {"none": 0, "silu": 0, "sigmoid_bias": 1, "softplus_bias": 1, "conv_silu": 2}


def _proj_kernel(u_ref, w_ref, *rest, epilogue, tiles_per_seq, n_cast):
    rest = list(rest)
    extra = [rest.pop(0) for _ in range(PROJ_EXTRA_INPUTS[epilogue])]
    cast_in = [rest.pop(0) for _ in range(n_cast)]
    o_ref = rest.pop(0)
    cast_out = [rest.pop(0) for _ in range(n_cast)]
    w_sc = rest.pop(0)
    i = pl.program_id(1)

    @pl.when(i == 0)
    def _():
        w = w_ref[...].reshape(w_ref.shape[-2:]).astype(BF16)
        rows = w.shape[0]
        for r in range(w_sc.shape[0] // rows):
            w_sc[r * rows:(r + 1) * rows, :] = w

    for src, dst in zip(cast_in, cast_out):
        dst[...] = src[...].astype(BF16)

    acc = _dot_nt(u_ref[...], w_sc[...])
    if epilogue == "silu":
        acc = acc * jax.nn.sigmoid(acc)
    elif epilogue == "sigmoid_bias":
        acc = jax.nn.sigmoid(acc + extra[0][...])
    elif epilogue == "softplus_bias":
        acc = _softplus(acc + extra[0][...])
    elif epilogue == "conv_silu":
        convw_ref, convb_ref = extra
        carry_sc = rest.pop(0)
        tm = acc.shape[0]
        carry = jnp.where(i % tiles_per_seq == 0, 0.0, carry_sc[...])
        carry_sc[...] = acc[tm - CONV_CARRY:]
        assert SSD_CONV == 4
        ext = jnp.concatenate([carry, acc], axis=0)
        ext1 = pltpu.roll(ext, 1, axis=0)
        far = pltpu.roll(convw_ref[1:2, :] * ext + convw_ref[0:1, :] * ext1, 2, axis=0)
        conv = (convb_ref[...] + convw_ref[3:4, :] * acc + convw_ref[2:3, :] * ext1[CONV_CARRY:]
                + far[CONV_CARRY:])
        acc = conv * jax.nn.sigmoid(conv)
    o_ref[...] = acc.astype(o_ref.dtype)


def _proj(u, wt, col_start, n, epilogue, extra=(), to_cast=(), *, seq, layer, tm=1024, tn=1024, repeat=1,
          out_dtype=F32):
    m, k = u.shape
    tn = min(tn, n)
    assert n % tn == 0 and m % tm == 0 and seq % tm == 0 and (repeat == 1 or n == tn)
    steps_j, steps_i = n // tn, m // tm
    if col_start % tn == 0:
        col_block = col_start // tn
        w_spec = pl.BlockSpec((None, tn, k), lambda j, i: (layer, col_block + j, 0))
    else:
        assert col_start % F32_SUBLANES == 0 and tn % F32_SUBLANES == 0
        w_spec = pl.BlockSpec((pl.Element(1), pl.Element(tn), pl.Element(k)),
                              lambda j, i: (layer, pl.multiple_of(col_start + j * tn, F32_SUBLANES), 0))
    to = tn * repeat
    in_specs = [pl.BlockSpec((tm, k), lambda j, i: (i, 0)), w_spec]
    for e in extra:
        in_specs.append(pl.BlockSpec((e.shape[0], to), lambda j, i: (0, j)))
    out_specs = [pl.BlockSpec((tm, to), lambda j, i: (i, j))]
    out_shape = [jax.ShapeDtypeStruct((m, n * repeat), out_dtype)]
    for arr in to_cast:
        rows = arr.shape[0] // (steps_j * steps_i)
        assert rows * steps_j * steps_i == arr.shape[0] and rows % BF16_SUBLANES == 0, arr.shape
        spec = pl.BlockSpec((rows, arr.shape[1]), lambda j, i: (j * steps_i + i, 0))
        in_specs.append(spec)
        out_specs.append(spec)
        out_shape.append(jax.ShapeDtypeStruct(arr.shape, BF16))
    scratch = [pltpu.VMEM((to, k), BF16)]
    if epilogue == "conv_silu":
        scratch.append(pltpu.VMEM((CONV_CARRY, to), F32))
    outs = pl.pallas_call(
        functools.partial(_proj_kernel, epilogue=epilogue, tiles_per_seq=seq // tm, n_cast=len(to_cast)),
        grid=(steps_j, steps_i),
        in_specs=in_specs,
        out_specs=out_specs,
        out_shape=out_shape,
        scratch_shapes=scratch,
        compiler_params=_params(("parallel", "arbitrary")),
        name="proj_" + epilogue,
    )(u, wt, *extra, *to_cast)
    return outs if to_cast else outs[0]


def _pool_kernel(p_ref, halo_ref, ga_ref, pw_ref, scale_ref, wo_ref, o_ref, mixed_sc, *, tiles_per_seq):
    tm = p_ref.shape[0]
    i = pl.program_id(0)
    tile_in_seq = i % tiles_per_seq
    halo = jnp.where(tile_in_seq == 0, 0.0, halo_ref[...])
    pos = tile_in_seq * tm + lax.broadcasted_iota(jnp.int32, (tm, 1), 0)
    group_cols = [slice(g * POOL_GROUP_DIM, (g + 1) * POOL_GROUP_DIM) for g in range(len(POOL_WINDOWS))]
    pooled = []
    for cols, window in zip(group_cols, POOL_WINDOWS):
        x = p_ref[:, cols]
        s = jnp.concatenate([halo[:, cols], x], axis=0)
        width = 1
        while width < window:
            s = s[width:] + s[:-width]
            width *= 2
        win = s[POOL_HALO - window + 1:]
        inv_cnt = 1.0 / jnp.minimum(pos + 1, window).astype(F32)
        pooled.append((win * inv_cnt - x).astype(BF16))
    mixed = [_dot(pooled[g], pw_ref[g]) for g in range(len(POOL_WINDOWS))]
    for g, cols in enumerate(group_cols):
        mixed_sc[:, cols] = (mixed[g] * scale_ref[:, cols]).astype(BF16)
    o_ref[...] = (ga_ref[...].astype(F32) * _dot(mixed_sc[...], wo_ref[...])).astype(o_ref.dtype)


def _pool(p, gates, pool_w, pool_scale, w_out, seq, *, tm=512):
    m, d = p.shape
    halo_blocks = tm // POOL_HALO
    row = pl.BlockSpec((tm, d), lambda i: (i, 0))
    return pl.pallas_call(
        functools.partial(_pool_kernel, tiles_per_seq=seq // tm),
        grid=(m // tm,),
        in_specs=[row,
                  pl.BlockSpec((POOL_HALO, d), lambda i: (jnp.maximum(i * halo_blocks - 1, 0), 0)),
                  row,
                  _const_spec(pool_w.shape),
                  _const_spec((1, d)),
                  _const_spec(w_out.shape)],
        out_specs=row,
        out_shape=jax.ShapeDtypeStruct((m, d), BF16),
        scratch_shapes=[pltpu.VMEM((tm, d), BF16)],
        compiler_params=_params(("parallel",)),
        name="pool",
    )(p, p, gates, pool_w, pool_scale, w_out)


HEADS_PER_DOT = MXU_DIM // SSD_HEAD_DIM
SSD_CHUNKS_PER_STEP = 4


def _ssd_constants():
    lane = np.arange(SSD_INNER)
    head_of_lane = lane // SSD_HEAD_DIM
    pos_of_lane = lane % SSD_HEAD_DIM
    expand = np.zeros((MXU_DIM, SSD_INNER), np.float32)
    for piece in range(3):
        expand[piece * SSD_HEADS + head_of_lane, lane] = 1.0
    row = np.arange(CHUNK)[:, None]
    diag = (row == pos_of_lane[None, :]).astype(np.float32)
    causal = np.where(row >= pos_of_lane[None, :], 0.0, -np.inf).astype(np.float32)
    blk = np.arange(MXU_DIM) // SSD_HEAD_DIM
    blockdiag = (blk[:, None] == blk[None, :]).astype(np.float32)
    return (jnp.asarray(expand, BF16), jnp.asarray(diag), jnp.asarray(causal), jnp.asarray(blockdiag, BF16))


def _split_select(v, lower_half, pieces):
    p0 = v.astype(BF16).astype(F32)
    r1 = v - p0
    p1 = r1.astype(BF16).astype(F32)
    first = jnp.where(lower_half, p0, p1).astype(BF16)
    if pieces == 2:
        return first
    p2 = (r1 - p1).astype(BF16)
    return jnp.concatenate([first, p2], axis=1)


def _ssd_kernel(xbc_ref, dt_ref, alog_ref, dskip_ref, expand_ref, diag_ref, causal_ref, bd_ref, y_ref,
                state_sc, colb_sc, dtx_sc):
    @pl.when(pl.program_id(1) == 0)
    def _():
        state_sc[...] = jnp.zeros_like(state_sc)

    dt = dt_ref[...]
    a = dt * (-jnp.exp(alog_ref[...]) * LOG2_E)
    row_in_chunk = lax.broadcasted_iota(jnp.int32, a.shape, 0) % CHUNK
    lower_half = lax.broadcasted_iota(jnp.int32, a.shape, 1) < SSD_HEADS
    acs = a
    shift = 1
    while shift < CHUNK:
        acs = acs + jnp.where(row_in_chunk >= shift, pltpu.roll(acs, shift, axis=0), 0.0)
        shift *= 2

    colb_sc[...] = _dot(_split_select(acs, lower_half, 3), expand_ref[...])
    dtx_sc[...] = _dot(_split_select(dt, lower_half, 2), expand_ref[0:LANES, :])

    groups = range(SSD_GROUPS)
    group_cols = [slice(g * SSD_GROUP_WIDTH, (g + 1) * SSD_GROUP_WIDTH) for g in groups]
    parts = range(SSD_GROUP_WIDTH // MXU_DIM)
    for q in range(SSD_CHUNKS_PER_STEP):
        rows = slice(q * CHUNK, (q + 1) * CHUNK)
        b_g = [xbc_ref[rows, SSD_INNER + g * SSD_STATE:SSD_INNER + (g + 1) * SSD_STATE].astype(BF16)
               for g in groups]
        c_g = [xbc_ref[rows, SSD_INNER + SSD_BC + g * SSD_STATE:SSD_INNER + SSD_BC + (g + 1) * SSD_STATE]
               .astype(BF16) for g in groups]
        cb2 = [_dot_nt(c_g[g], jnp.concatenate([b_g[g], b_g[g]], axis=0)) for g in groups]
        y_prev = [_dot(c_g[g], state_sc[g].astype(BF16)) for g in groups]

        gmat, x_bd, xend_b, colb = [], [], [], []
        for g in groups:
            cols = group_cols[g]
            colb.append(colb_sc[rows, cols])
            rowb = jnp.sum(colb[g] * diag_ref[:, cols], axis=0, keepdims=True)
            decay_ls = jnp.exp2(colb[g] - rowb + causal_ref[:, cols])
            xdt = xbc_ref[rows, cols] * dtx_sc[rows, cols]
            xdt_b = xdt.astype(BF16)
            xend_b.append((xdt * jnp.exp2(colb[g][CHUNK - 1:CHUNK, :] - colb[g])).astype(BF16))
            cb = jnp.concatenate([cb2[g]] * (SSD_GROUP_WIDTH // LANES), axis=1)
            gmat.append((cb * decay_ls).astype(BF16))
            x_bd.append([jnp.concatenate([xdt_b[:, part * MXU_DIM:(part + 1) * MXU_DIM]] * HEADS_PER_DOT, axis=0)
                         * bd_ref[...] for part in parts])

        y_diag = [jnp.concatenate([_dot(gmat[g][:, part * MXU_DIM:(part + 1) * MXU_DIM], x_bd[g][part])
                                   for part in parts], axis=1) for g in groups]
        new_state = [lax.dot_general(b_g[g], xend_b[g], (((0,), (0,)), ((), ())), preferred_element_type=F32)
                     for g in groups]

        for g in groups:
            cols = group_cols[g]
            y_ref[rows, cols] = (y_prev[g] * jnp.exp2(colb[g]) + dskip_ref[:, cols] * xbc_ref[rows, cols]
                                 + y_diag[g]).astype(y_ref.dtype)
            state_sc[g] = state_sc[g] * jnp.exp2(colb[g][CHUNK - 1:CHUNK, :]) + new_state[g]


def _ssd(xbc, dt2, a_log2, dskip_x, batch):
    m = xbc.shape[0]
    rows = SSD_CHUNKS_PER_STEP * CHUNK
    steps = m // batch // rows
    assert steps * rows * batch == m
    expand, diag, causal, blockdiag = _ssd_constants()
    tok = lambda width: pl.BlockSpec((rows, width), lambda b, c: (b * steps + c, 0))
    return pl.pallas_call(
        _ssd_kernel,
        grid=(batch, steps),
        in_specs=[tok(SSD_CONV_DIM), tok(LANES), _const_spec(a_log2.shape),
                  _const_spec(dskip_x.shape), _const_spec(expand.shape), _const_spec(diag.shape),
                  _const_spec(causal.shape), _const_spec(blockdiag.shape)],
        out_specs=tok(SSD_INNER),
        out_shape=jax.ShapeDtypeStruct((m, SSD_INNER), BF16),
        scratch_shapes=[pltpu.VMEM((SSD_GROUPS, SSD_STATE, SSD_GROUP_WIDTH), F32),
                        pltpu.VMEM((rows, SSD_INNER), F32), pltpu.VMEM((rows, SSD_INNER), F32)],
        compiler_params=_params(("arbitrary", "arbitrary")),
        name="ssd",
    )(xbc, dt2, a_log2, dskip_x, expand, diag, causal, blockdiag)


def _merge_kernel(y_ref, z_ref, ng_ref, wso_ref, gaya_ref, gb_ref, wmix_ref, post_ref, h_ref, next_ref,
                  ho_ref, un_ref, yn_sc):
    yz = y_ref[...].astype(F32) * z_ref[...].astype(F32)
    for g in range(SSD_GROUPS):
        cols = slice(g * SSD_GROUP_WIDTH, (g + 1) * SSD_GROUP_WIDTH)
        yn_sc[:, cols] = _rms(yz[:, cols], ng_ref[:, cols]).astype(BF16)
    y_b = _dot(yn_sc[...], wso_ref[...])
    merged = (gaya_ref[...].astype(F32) + gb_ref[...].astype(F32) * y_b).astype(BF16)
    h = h_ref[...] + _rms(_dot(merged, wmix_ref[...]), post_ref[...])
    ho_ref[...] = h
    un_ref[...] = _rms(h, next_ref[...]).astype(BF16)


def _merge(y, z, norm_g, w_ssd_out, gaya, gates, w_mix, post_g, h, next_g, *, tm=256):
    m, d = h.shape
    wide = pl.BlockSpec((tm, SSD_INNER), lambda i: (i, 0))
    row = pl.BlockSpec((tm, d), lambda i: (i, 0))
    return pl.pallas_call(
        _merge_kernel,
        grid=(m // tm,),
        in_specs=[wide, wide, _const_spec((1, SSD_INNER)), _const_spec(w_ssd_out.shape),
                  row,
                  pl.BlockSpec((tm, d), lambda i: (i, 1)),
                  _const_spec(w_mix.shape), _const_spec((1, d)), row, _const_spec((1, d))],
        out_specs=[row, row],
        out_shape=[jax.ShapeDtypeStruct((m, d), F32), jax.ShapeDtypeStruct((m, d), BF16)],
        scratch_shapes=[pltpu.VMEM((tm, SSD_INNER), BF16)],
        compiler_params=_params(("parallel",)),
        name="merge",
    )(y, z, norm_g, w_ssd_out, gaya, gates, w_mix, post_g, h, next_g)


def _kv_kernel(mem_ref, g_ref, wk_ref, wv_ref, k_ref, v_ref):
    mem_n = _rms(mem_ref[...], g_ref[...]).astype(BF16)
    k_ref[...] = _dot(mem_n, wk_ref[...]).astype(BF16)
    v_ref[...] = _dot(mem_n, wv_ref[...]).astype(BF16)


def _kv(mem, g, w_k, w_v):
    rows = mem.shape[0]
    out = jax.ShapeDtypeStruct((rows, X_WIDTH), BF16)
    return pl.pallas_call(_kv_kernel, out_shape=[out, out], name="kv",
                          compiler_params=pltpu.CompilerParams(vmem_limit_bytes=V7X_VMEM_LIMIT))(mem, g, w_k, w_v)


def _attn_kernel(u_ref, h_ref, wq_ref, k_ref, v_ref, wo_ref, post_ref, ho_ref, o_sc):
    q = _dot(u_ref[...], wq_ref[...]).astype(BF16)
    heads = range(X_HEADS)
    head_cols = [slice(hd * X_HEAD_DIM, (hd + 1) * X_HEAD_DIM) for hd in heads]
    scores = [_dot_nt(q[:, cols], k_ref[:, cols]) * (1.0 / math.sqrt(X_HEAD_DIM)) for cols in head_cols]
    weights = [jnp.exp(s - jnp.max(s, axis=-1, keepdims=True)) for s in scores]
    denom = [jnp.sum(e, axis=-1, keepdims=True) for e in weights]
    mixed = [_dot(weights[hd].astype(BF16), v_ref[:, head_cols[hd]]) for hd in heads]
    for hd in heads:
        o_sc[:, head_cols[hd]] = (mixed[hd] / denom[hd]).astype(BF16)
    ho_ref[...] = h_ref[...] + _rms(_dot(o_sc[...], wo_ref[...]), post_ref[...])


def _attn(u, h, w_q, k, v, w_o, post_g, seq, *, tm=512):
    m, d = h.shape
    tiles_per_seq = seq // tm
    row = pl.BlockSpec((tm, d), lambda i: (i, 0))
    mem = pl.BlockSpec((N_MEM, X_WIDTH), lambda i: (i // tiles_per_seq, 0))
    return pl.pallas_call(
        _attn_kernel,
        grid=(m // tm,),
        in_specs=[row, row, _const_spec(w_q.shape), mem, mem, _const_spec(w_o.shape),
                  _const_spec((1, d))],
        out_specs=row,
        out_shape=jax.ShapeDtypeStruct((m, d), F32),
        scratch_shapes=[pltpu.VMEM((tm, X_WIDTH), BF16)],
        compiler_params=_params(("parallel",)),
        name="attn",
    )(u, h, w_q, k, v, w_o, post_g)


def kernel(x, mem, ffn1_pre_g, ffn1_w1, ffn1_w3, ffn1_w2, ffn1_post_g, mix_pre_g, w_in, b_gate, pool_w, pool_scale, w_pool_out, conv_w, conv_b, dt_bias, a_log, d_skip, ssd_norm_g, w_ssd_out, w_mix_out, mix_post_g, xattn_pre_g, mem_norm_g, w_q, w_k, w_v, w_xo, xattn_post_g, ffn2_pre_g, ffn2_w1, ffn2_w3, ffn2_w2, ffn2_post_g):
    batch, seq, d = x.shape
    depth = ffn1_w1.shape[0]
    c0 = D_MODEL
    c1 = c0 + SSD_INNER
    c2 = c1 + SSD_CONV_DIM
    c3 = c2 + SSD_HEADS
    vec = lambda g: g.reshape(1, -1).astype(F32)
    twice = lambda t: jnp.concatenate([t, t], axis=-1)

    h = x.reshape(batch * seq, d)
    mem2 = mem.reshape(batch * N_MEM, d)
    w_in_t = jnp.swapaxes(w_in, 1, 2)
    u = None
    for i in range(depth):
        ffn1 = (vec(ffn1_pre_g[i]), ffn1_w1[i], ffn1_w3[i], ffn1_w2[i], 0.5 * vec(ffn1_post_g[i]),
                vec(mix_pre_g[i]))
        h_head, u_head, ffn1_wb = _ffn(h, *ffn1, tf=FFN_HEAD_TF, rows=FFN_TM)
        h, u, _ = _ffn(h, ffn1[0], *ffn1_wb, *ffn1[4:], head=(h_head, u_head))

        p_in, wb_q, wb_k, wb_v, wb_xo = _proj(u, w_in_t, 0, c0, "none", (), (w_q[i], w_k[i], w_v[i], w_xo[i]),
                                              seq=seq, layer=i)
        z_act, f2_w1, f2_w3, f2_w2 = _proj(u, w_in_t, c0, c1 - c0, "silu", (),
                                           (ffn2_w1[i], ffn2_w3[i], ffn2_w2[i]), seq=seq, layer=i,
                                           out_dtype=BF16)
        xbc = _proj(u, w_in_t, c1, c2 - c1, "conv_silu", (conv_w[i], vec(conv_b[i])), seq=seq, layer=i)
        dt2 = _proj(u, w_in_t, c2, c3 - c2, "softplus_bias", (twice(vec(dt_bias[i])),), seq=seq, layer=i,
                    repeat=2)
        gates, wb_ssd_out, wb_mix_out, wb_pool_out, wb_pool = _proj(
            u, w_in_t, c3, 2 * D_MODEL, "sigmoid_bias", (vec(b_gate[i]),),
            (w_ssd_out[i], w_mix_out[i], w_pool_out[i], pool_w[i].reshape(-1, POOL_GROUP_DIM)),
            seq=seq, layer=i, out_dtype=BF16)

        gaya = _pool(p_in, gates, wb_pool.reshape(pool_w.shape[1:]), vec(pool_scale[i]), wb_pool_out, seq)
        y = _ssd(xbc, dt2, twice(vec(a_log[i])), jnp.repeat(vec(d_skip[i]), SSD_HEAD_DIM, axis=1), batch)
        h, u = _merge(y, z_act, vec(ssd_norm_g[i]), wb_ssd_out, gaya, gates, wb_mix_out,
                      vec(mix_post_g[i]), h, vec(xattn_pre_g[i]))

        k, v = _kv(mem2, vec(mem_norm_g[i]), wb_k, wb_v)
        h = _attn(u, h, wb_q, k, v, wb_xo, vec(xattn_post_g[i]), seq)

        last = i == depth - 1
        h, u, _ = _ffn(h, vec(ffn2_pre_g[i]), f2_w1, f2_w3, f2_w2, 0.5 * vec(ffn2_post_g[i]),
                       None if last else vec(ffn1_pre_g[i + 1]))
    return h.reshape(batch, seq, d)
```

```python
import functools
import math

import jax
import jax.numpy as jnp
import numpy as np
from jax import lax
from jax.experimental import pallas as pl
from jax.experimental.pallas import tpu as pltpu

F32 = jnp.float32
BF16 = jnp.bfloat16

D_MODEL = 2048
D_FF = 5632
CHUNK = 64
N_MEM = 256
POOL_WINDOWS = (2, 4, 8, 16)
POOL_GROUP_DIM = D_MODEL // len(POOL_WINDOWS)
POOL_HALO = 16
SSD_INNER = 4096
SSD_HEAD_DIM = 64
SSD_HEADS = SSD_INNER // SSD_HEAD_DIM
SSD_GROUPS = 8
SSD_STATE = 128
SSD_GROUP_WIDTH = SSD_INNER // SSD_GROUPS
SSD_BC = SSD_GROUPS * SSD_STATE
SSD_CONV = 4
SSD_CONV_DIM = SSD_INNER + 2 * SSD_BC
X_HEADS = 4
X_HEAD_DIM = 128
X_WIDTH = X_HEADS * X_HEAD_DIM
EPS = 1e-6
LOG2_E = math.log2(math.e)

LANES = 128
F32_SUBLANES = 8
BF16_SUBLANES = 16
MXU_DIM = 256
V7X_VMEM_LIMIT = 60 * 1024 * 1024


def _params(semantics):
    return pltpu.CompilerParams(dimension_semantics=semantics, vmem_limit_bytes=V7X_VMEM_LIMIT)


def _rms(x, g):
    return x * lax.rsqrt(jnp.mean(x * x, axis=-1, keepdims=True) + EPS) * g


def _dot(a, b):
    return jnp.dot(a, b, preferred_element_type=F32)


def _dot_nt(a, b):
    return lax.dot_general(a, b, (((1,), (1,)), ((), ())), preferred_element_type=F32)


def _const_spec(shape):
    return pl.BlockSpec(shape, lambda *_: (0,) * len(shape), pipeline_mode=pl.Buffered(1))


def _ffn_kernel(x_ref, pre_ref, w1_ref, w3_ref, w2_ref, post_ref, *rest, emit_next, emit_weights, has_head):
    rest = list(rest)
    next_ref = rest.pop(0) if emit_next else None
    head_refs = [rest.pop(0) for _ in range(2 if emit_next else 1)] if has_head else None
    h_ref = rest.pop(0)
    un_ref = rest.pop(0) if emit_next else None
    wb_refs = [rest.pop(0) for _ in range(3)] if emit_weights else None
    u_sc, acc_sc = rest
    i, j = pl.program_id(0), pl.program_id(1)
    last = pl.num_programs(1) - 1

    def body():
        @pl.when(j == 0)
        def _():
            u_sc[...] = _rms(x_ref[...], pre_ref[...]).astype(BF16)
            acc_sc[...] = jnp.zeros_like(acc_sc)

        w1, w3, w2 = w1_ref[...], w3_ref[...], w2_ref[...]
        if emit_weights:
            w1, w3, w2 = w1.astype(BF16), w3.astype(BF16), w2.astype(BF16)
            for dst, w in zip(wb_refs, (w1, w3, w2)):
                dst[...] = w
        u = u_sc[...]
        a = _dot(u, w1)
        b = _dot(u, w3)
        act = (a * jax.nn.sigmoid(a) * b).astype(BF16)
        acc_sc[...] += _dot(act, w2)

        @pl.when(j == last)
        def _():
            h = x_ref[...] + _rms(acc_sc[...], post_ref[...])
            h_ref[...] = h
            if emit_next:
                un_ref[...] = _rms(h, next_ref[...]).astype(BF16)

    if has_head:
        @pl.when(jnp.logical_and(i == 0, j == last))
        def _():
            h_ref[...] = head_refs[0][...]
            if emit_next:
                un_ref[...] = head_refs[1][...]

        pl.when(i > 0)(body)
    else:
        body()


FFN_TM = 512
FFN_TF = 512
FFN_HEAD_TF = 256


def _ffn(x, pre_g, w1, w3, w2, post_g, next_g, *, tm=FFN_TM, tf=FFN_TF, rows=None, head=None):
    m, d = x.shape
    rows = m if rows is None else rows
    dff = w1.shape[1]
    emit_next = next_g is not None
    emit_weights = w1.dtype != BF16
    has_head = head is not None
    jw = (lambda i, j: jnp.where(i == 0, 0, j)) if has_head else (lambda i, j: j)
    row = pl.BlockSpec((tm, d), lambda i, j: (i, 0))
    vec = pl.BlockSpec((1, d), lambda i, j: (0, 0))
    w_up = pl.BlockSpec((d, tf), lambda i, j: (0, jw(i, j)))
    w_down = pl.BlockSpec((tf, d), lambda i, j: (jw(i, j), 0))
    in_specs = [row, vec, w_up, w_up, w_down, vec]
    args = [x, pre_g, w1, w3, w2, post_g]
    out_shape = [jax.ShapeDtypeStruct((rows, d), F32)]
    out_specs = [row]
    if emit_next:
        in_specs.append(vec)
        args.append(next_g)
        out_shape.append(jax.ShapeDtypeStruct((rows, d), BF16))
        out_specs.append(row)
    if has_head:
        for t in head[:2 if emit_next else 1]:
            in_specs.append(_const_spec(t.shape))
            args.append(t)
    if emit_weights:
        out_shape += [jax.ShapeDtypeStruct(w.shape, BF16) for w in (w1, w3, w2)]
        out_specs += [w_up, w_up, w_down]
    outs = pl.pallas_call(
        functools.partial(_ffn_kernel, emit_next=emit_next, emit_weights=emit_weights, has_head=has_head),
        grid=(rows // tm, dff // tf),
        in_specs=in_specs,
        out_specs=out_specs,
        out_shape=out_shape,
        scratch_shapes=[pltpu.VMEM((tm, d), BF16), pltpu.VMEM((tm, d), F32)],
        compiler_params=_params(("parallel", "arbitrary")),
        name="ffn",
    )(*args)
    n_main = 2 if emit_next else 1
    return outs[0], (outs[1] if emit_next else None), (tuple(outs[n_main:]) if emit_weights else None)


def _softplus(x):
    return jnp.maximum(x, 0.0) + jnp.log1p(jnp.exp(-jnp.abs(x)))


CONV_CARRY = 8
PROJ_EXTRA_INPUTS = {"none": 0, "silu": 0, "sigmoid_bias": 1, "softplus_bias": 1, "conv_silu": 2}


def _proj_kernel(u_ref, w_ref, *rest, epilogue, tiles_per_seq, n_cast):
    rest = list(rest)
    extra = [rest.pop(0) for _ in range(PROJ_EXTRA_INPUTS[epilogue])]
    cast_in = [rest.pop(0) for _ in range(n_cast)]
    o_ref = rest.pop(0)
    cast_out = [rest.pop(0) for _ in range(n_cast)]
    w_sc = rest.pop(0)
    i = pl.program_id(1)

    @pl.when(i == 0)
    def _():
        w = w_ref[...].reshape(w_ref.shape[-2:]).astype(BF16)
        rows = w.shape[0]
        for r in range(w_sc.shape[0] // rows):
            w_sc[r * rows:(r + 1) * rows, :] = w

    for src, dst in zip(cast_in, cast_out):
        dst[...] = src[...].astype(BF16)

    acc = _dot_nt(u_ref[...], w_sc[...])
    if epilogue == "silu":
        acc = acc * jax.nn.sigmoid(acc)
    elif epilogue == "sigmoid_bias":
        acc = jax.nn.sigmoid(acc + extra[0][...])
    elif epilogue == "softplus_bias":
        acc = _softplus(acc + extra[0][...])
    elif epilogue == "conv_silu":
        convw_ref, convb_ref = extra
        carry_sc = rest.pop(0)
        tm = acc.shape[0]
        carry = jnp.where(i % tiles_per_seq == 0, 0.0, carry_sc[...])
        carry_sc[...] = acc[tm - CONV_CARRY:]
        assert SSD_CONV == 4
        ext = jnp.concatenate([carry, acc], axis=0)
        ext1 = pltpu.roll(ext, 1, axis=0)
        far = pltpu.roll(convw_ref[1:2, :] * ext + convw_ref[0:1, :] * ext1, 2, axis=0)
        conv = (convb_ref[...] + convw_ref[3:4, :] * acc + convw_ref[2:3, :] * ext1[CONV_CARRY:]
                + far[CONV_CARRY:])
        acc = conv * jax.nn.sigmoid(conv)
    o_ref[...] = acc.astype(o_ref.dtype)


def _proj(u, wt, col_start, n, epilogue, extra=(), to_cast=(), *, seq, layer, tm=1024, tn=1024, repeat=1,
          out_dtype=F32):
    m, k = u.shape
    tn = min(tn, n)
    assert n % tn == 0 and m % tm == 0 and seq % tm == 0 and (repeat == 1 or n == tn)
    steps_j, steps_i = n // tn, m // tm
    if col_start % tn == 0:
        col_block = col_start // tn
        w_spec = pl.BlockSpec((None, tn, k), lambda j, i: (layer, col_block + j, 0))
    else:
        assert col_start % F32_SUBLANES == 0 and tn % F32_SUBLANES == 0
        w_spec = pl.BlockSpec((pl.Element(1), pl.Element(tn), pl.Element(k)),
                              lambda j, i: (layer, pl.multiple_of(col_start + j * tn, F32_SUBLANES), 0))
    to = tn * repeat
    in_specs = [pl.BlockSpec((tm, k), lambda j, i: (i, 0)), w_spec]
    for e in extra:
        in_specs.append(pl.BlockSpec((e.shape[0], to), lambda j, i: (0, j)))
    out_specs = [pl.BlockSpec((tm, to), lambda j, i: (i, j))]
    out_shape = [jax.ShapeDtypeStruct((m, n * repeat), out_dtype)]
    for arr in to_cast:
        rows = arr.shape[0] // (steps_j * steps_i)
        assert rows * steps_j * steps_i == arr.shape[0] and rows % BF16_SUBLANES == 0, arr.shape
        spec = pl.BlockSpec((rows, arr.shape[1]), lambda j, i: (j * steps_i + i, 0))
        in_specs.append(spec)
        out_specs.append(spec)
        out_shape.append(jax.ShapeDtypeStruct(arr.shape, BF16))
    scratch = [pltpu.VMEM((to, k), BF16)]
    if epilogue == "conv_silu":
        scratch.append(pltpu.VMEM((CONV_CARRY, to), F32))
    outs = pl.pallas_call(
        functools.partial(_proj_kernel, epilogue=epilogue, tiles_per_seq=seq // tm, n_cast=len(to_cast)),
        grid=(steps_j, steps_i),
        in_specs=in_specs,
        out_specs=out_specs,
        out_shape=out_shape,
        scratch_shapes=scratch,
        compiler_params=_params(("parallel", "arbitrary")),
        name="proj_" + epilogue,
    )(u, wt, *extra, *to_cast)
    return outs if to_cast else outs[0]


def _pool_kernel(p_ref, halo_ref, ga_ref, pw_ref, scale_ref, wo_ref, o_ref, mixed_sc, *, tiles_per_seq):
    tm = p_ref.shape[0]
    i = pl.program_id(0)
    tile_in_seq = i % tiles_per_seq
    halo = jnp.where(tile_in_seq == 0, 0.0, halo_ref[...])
    pos = tile_in_seq * tm + lax.broadcasted_iota(jnp.int32, (tm, 1), 0)
    group_cols = [slice(g * POOL_GROUP_DIM, (g + 1) * POOL_GROUP_DIM) for g in range(len(POOL_WINDOWS))]
    pooled = []
    for cols, window in zip(group_cols, POOL_WINDOWS):
        x = p_ref[:, cols]
        s = jnp.concatenate([halo[:, cols], x], axis=0)
        width = 1
        while width < window:
            s = s[width:] + s[:-width]
            width *= 2
        win = s[POOL_HALO - window + 1:]
        inv_cnt = 1.0 / jnp.minimum(pos + 1, window).astype(F32)
        pooled.append((win * inv_cnt - x).astype(BF16))
    mixed = [_dot(pooled[g], pw_ref[g]) for g in range(len(POOL_WINDOWS))]
    for g, cols in enumerate(group_cols):
        mixed_sc[:, cols] = (mixed[g] * scale_ref[:, cols]).astype(BF16)
    o_ref[...] = (ga_ref[...].astype(F32) * _dot(mixed_sc[...], wo_ref[...])).astype(o_ref.dtype)


def _pool(p, gates, pool_w, pool_scale, w_out, seq, *, tm=512):
    m, d = p.shape
    halo_blocks = tm // POOL_HALO
    row = pl.BlockSpec((tm, d), lambda i: (i, 0))
    return pl.pallas_call(
        functools.partial(_pool_kernel, tiles_per_seq=seq // tm),
        grid=(m // tm,),
        in_specs=[row,
                  pl.BlockSpec((POOL_HALO, d), lambda i: (jnp.maximum(i * halo_blocks - 1, 0), 0)),
                  row,
                  _const_spec(pool_w.shape),
                  _const_spec((1, d)),
                  _const_spec(w_out.shape)],
        out_specs=row,
        out_shape=jax.ShapeDtypeStruct((m, d), BF16),
        scratch_shapes=[pltpu.VMEM((tm, d), BF16)],
        compiler_params=_params(("parallel",)),
        name="pool",
    )(p, p, gates, pool_w, pool_scale, w_out)


HEADS_PER_DOT = MXU_DIM // SSD_HEAD_DIM
SSD_CHUNKS_PER_STEP = 4


def _ssd_constants():
    lane = np.arange(SSD_INNER)
    head_of_lane = lane // SSD_HEAD_DIM
    pos_of_lane = lane % SSD_HEAD_DIM
    expand = np.zeros((MXU_DIM, SSD_INNER), np.float32)
    for piece in range(3):
        expand[piece * SSD_HEADS + head_of_lane, lane] = 1.0
    row = np.arange(CHUNK)[:, None]
    diag = (row == pos_of_lane[None, :]).astype(np.float32)
    causal = np.where(row >= pos_of_lane[None, :], 0.0, -np.inf).astype(np.float32)
    blk = np.arange(MXU_DIM) // SSD_HEAD_DIM
    blockdiag = (blk[:, None] == blk[None, :]).astype(np.float32)
    return (jnp.asarray(expand, BF16), jnp.asarray(diag), jnp.asarray(causal), jnp.asarray(blockdiag, BF16))


def _split_select(v, lower_half, pieces):
    p0 = v.astype(BF16).astype(F32)
    r1 = v - p0
    p1 = r1.astype(BF16).astype(F32)
    first = jnp.where(lower_half, p0, p1).astype(BF16)
    if pieces == 2:
        return first
    p2 = (r1 - p1).astype(BF16)
    return jnp.concatenate([first, p2], axis=1)


def _ssd_kernel(xbc_ref, dt_ref, alog_ref, dskip_ref, expand_ref, diag_ref, causal_ref, bd_ref, y_ref,
                state_sc, colb_sc, dtx_sc):
    @pl.when(pl.program_id(1) == 0)
    def _():
        state_sc[...] = jnp.zeros_like(state_sc)

    dt = dt_ref[...]
    a = dt * (-jnp.exp(alog_ref[...]) * LOG2_E)
    row_in_chunk = lax.broadcasted_iota(jnp.int32, a.shape, 0) % CHUNK
    lower_half = lax.broadcasted_iota(jnp.int32, a.shape, 1) < SSD_HEADS
    acs = a
    shift = 1
    while shift < CHUNK:
        acs = acs + jnp.where(row_in_chunk >= shift, pltpu.roll(acs, shift, axis=0), 0.0)
        shift *= 2

    colb_sc[...] = _dot(_split_select(acs, lower_half, 3), expand_ref[...])
    dtx_sc[...] = _dot(_split_select(dt, lower_half, 2), expand_ref[0:LANES, :])

    groups = range(SSD_GROUPS)
    group_cols = [slice(g * SSD_GROUP_WIDTH, (g + 1) * SSD_GROUP_WIDTH) for g in groups]
    parts = range(SSD_GROUP_WIDTH // MXU_DIM)
    chunk_rows = [slice(q * CHUNK, (q + 1) * CHUNK) for q in range(SSD_CHUNKS_PER_STEP)]

    def scores(q):
        rows = chunk_rows[q]
        b_g = [xbc_ref[rows, SSD_INNER + g * SSD_STATE:SSD_INNER + (g + 1) * SSD_STATE].astype(BF16)
               for g in groups]
        c_g = [xbc_ref[rows, SSD_INNER + SSD_BC + g * SSD_STATE:SSD_INNER + SSD_BC + (g + 1) * SSD_STATE]
               .astype(BF16) for g in groups]
        cb2 = [_dot_nt(c_g[g], jnp.concatenate([b_g[g], b_g[g]], axis=0)) for g in groups]
        return b_g, c_g, cb2

    def decays(q, cb2):
        rows = chunk_rows[q]
        gmat, x_bd, xend_b = [], [], []
        for g in groups:
            cols = group_cols[g]
            colb = colb_sc[rows, cols]
            rowb = jnp.sum(colb * diag_ref[:, cols], axis=0, keepdims=True)
            decay_ls = jnp.exp2(colb - rowb + causal_ref[:, cols])
            xdt = xbc_ref[rows, cols] * dtx_sc[rows, cols]
            xdt_b = xdt.astype(BF16)
            xend_b.append((xdt * jnp.exp2(colb[CHUNK - 1:CHUNK, :] - colb)).astype(BF16))
            cb = jnp.concatenate([cb2[g]] * (SSD_GROUP_WIDTH // LANES), axis=1)
            gmat.append((cb * decay_ls).astype(BF16))
            x_bd.append([jnp.concatenate([xdt_b[:, part * MXU_DIM:(part + 1) * MXU_DIM]] * HEADS_PER_DOT, axis=0)
                         * bd_ref[...] for part in parts])
        return gmat, x_bd, xend_b

    def products(b_g, c_g, gmat, x_bd, xend_b):
        y_prev = [_dot(c_g[g], state_sc[g].astype(BF16)) for g in groups]
        y_diag = [jnp.concatenate([_dot(gmat[g][:, part * MXU_DIM:(part + 1) * MXU_DIM], x_bd[g][part])
                                   for part in parts], axis=1) for g in groups]
        new_state = [lax.dot_general(b_g[g], xend_b[g], (((0,), (0,)), ((), ())), preferred_element_type=F32)
                     for g in groups]
        return y_prev, y_diag, new_state

    def finish(q, y_prev, y_diag, new_state):
        rows = chunk_rows[q]
        for g in groups:
            cols = group_cols[g]
            colb = colb_sc[rows, cols]
            y_ref[rows, cols] = (y_prev[g] * jnp.exp2(colb) + dskip_ref[:, cols] * xbc_ref[rows, cols]
                                 + y_diag[g]).astype(y_ref.dtype)
            state_sc[g] = state_sc[g] * jnp.exp2(colb[CHUNK - 1:CHUNK, :]) + new_state[g]

    for q in range(SSD_CHUNKS_PER_STEP):
        b_g, c_g, cb2 = scores(q)
        finish(q, *products(b_g, c_g, *decays(q, cb2)))


def _ssd(xbc, dt2, a_log2, dskip_x, batch):
    m = xbc.shape[0]
    rows = SSD_CHUNKS_PER_STEP * CHUNK
    steps = m // batch // rows
    assert steps * rows * batch == m
    expand, diag, causal, blockdiag = _ssd_constants()
    tok = lambda width: pl.BlockSpec((rows, width), lambda b, c: (b * steps + c, 0))
    return pl.pallas_call(
        _ssd_kernel,
        grid=(batch, steps),
        in_specs=[tok(SSD_CONV_DIM), tok(LANES), _const_spec(a_log2.shape),
                  _const_spec(dskip_x.shape), _const_spec(expand.shape), _const_spec(diag.shape),
                  _const_spec(causal.shape), _const_spec(blockdiag.shape)],
        out_specs=tok(SSD_INNER),
        out_shape=jax.ShapeDtypeStruct((m, SSD_INNER), BF16),
        scratch_shapes=[pltpu.VMEM((SSD_GROUPS, SSD_STATE, SSD_GROUP_WIDTH), F32),
                        pltpu.VMEM((rows, SSD_INNER), F32), pltpu.VMEM((rows, SSD_INNER), F32)],
        compiler_params=_params(("arbitrary", "arbitrary")),
        name="ssd",
    )(xbc, dt2, a_log2, dskip_x, expand, diag, causal, blockdiag)


def _merge_kernel(y_ref, z_ref, ng_ref, wso_ref, gaya_ref, gb_ref, wmix_ref, post_ref, h_ref, next_ref,
                  ho_ref, un_ref, yn_sc):
    yz = y_ref[...].astype(F32) * z_ref[...].astype(F32)
    for g in range(SSD_GROUPS):
        cols = slice(g * SSD_GROUP_WIDTH, (g + 1) * SSD_GROUP_WIDTH)
        yn_sc[:, cols] = _rms(yz[:, cols], ng_ref[:, cols]).astype(BF16)
    y_b = _dot(yn_sc[...], wso_ref[...])
    merged = (gaya_ref[...].astype(F32) + gb_ref[...].astype(F32) * y_b).astype(BF16)
    h = h_ref[...] + _rms(_dot(merged, wmix_ref[...]), post_ref[...])
    ho_ref[...] = h
    un_ref[...] = _rms(h, next_ref[...]).astype(BF16)


def _merge(y, z, norm_g, w_ssd_out, gaya, gates, w_mix, post_g, h, next_g, *, tm=256):
    m, d = h.shape
    wide = pl.BlockSpec((tm, SSD_INNER), lambda i: (i, 0))
    row = pl.BlockSpec((tm, d), lambda i: (i, 0))
    return pl.pallas_call(
        _merge_kernel,
        grid=(m // tm,),
        in_specs=[wide, wide, _const_spec((1, SSD_INNER)), _const_spec(w_ssd_out.shape),
                  row,
                  pl.BlockSpec((tm, d), lambda i: (i, 1)),
                  _const_spec(w_mix.shape), _const_spec((1, d)), row, _const_spec((1, d))],
        out_specs=[row, row],
        out_shape=[jax.ShapeDtypeStruct((m, d), F32), jax.ShapeDtypeStruct((m, d), BF16)],
        scratch_shapes=[pltpu.VMEM((tm, SSD_INNER), BF16)],
        compiler_params=_params(("parallel",)),
        name="merge",
    )(y, z, norm_g, w_ssd_out, gaya, gates, w_mix, post_g, h, next_g)


def _kv_kernel(mem_ref, g_ref, wk_ref, wv_ref, k_ref, v_ref):
    mem_n = _rms(mem_ref[...], g_ref[...]).astype(BF16)
    k_ref[...] = _dot(mem_n, wk_ref[...]).astype(BF16)
    v_ref[...] = _dot(mem_n, wv_ref[...]).astype(BF16)


def _kv(mem, g, w_k, w_v):
    rows = mem.shape[0]
    out = jax.ShapeDtypeStruct((rows, X_WIDTH), BF16)
    return pl.pallas_call(_kv_kernel, out_shape=[out, out], name="kv",
                          compiler_params=pltpu.CompilerParams(vmem_limit_bytes=V7X_VMEM_LIMIT))(mem, g, w_k, w_v)


def _attn_kernel(u_ref, h_ref, wq_ref, k_ref, v_ref, wo_ref, post_ref, ho_ref, o_sc):
    q = _dot(u_ref[...], wq_ref[...]).astype(BF16)
    heads = range(X_HEADS)
    head_cols = [slice(hd * X_HEAD_DIM, (hd + 1) * X_HEAD_DIM) for hd in heads]
    scores = [_dot_nt(q[:, cols], k_ref[:, cols]) * (1.0 / math.sqrt(X_HEAD_DIM)) for cols in head_cols]
    weights = [jnp.exp(s - jnp.max(s, axis=-1, keepdims=True)) for s in scores]
    denom = [jnp.sum(e, axis=-1, keepdims=True) for e in weights]
    mixed = [_dot(weights[hd].astype(BF16), v_ref[:, head_cols[hd]]) for hd in heads]
    for hd in heads:
        o_sc[:, head_cols[hd]] = (mixed[hd] / denom[hd]).astype(BF16)
    ho_ref[...] = h_ref[...] + _rms(_dot(o_sc[...], wo_ref[...]), post_ref[...])


def _attn(u, h, w_q, k, v, w_o, post_g, seq, *, tm=512):
    m, d = h.shape
    tiles_per_seq = seq // tm
    row = pl.BlockSpec((tm, d), lambda i: (i, 0))
    mem = pl.BlockSpec((N_MEM, X_WIDTH), lambda i: (i // tiles_per_seq, 0))
    return pl.pallas_call(
        _attn_kernel,
        grid=(m // tm,),
        in_specs=[row, row, _const_spec(w_q.shape), mem, mem, _const_spec(w_o.shape),
                  _const_spec((1, d))],
        out_specs=row,
        out_shape=jax.ShapeDtypeStruct((m, d), F32),
        scratch_shapes=[pltpu.VMEM((tm, X_WIDTH), BF16)],
        compiler_params=_params(("parallel",)),
        name="attn",
    )(u, h, w_q, k, v, w_o, post_g)


def kernel(x, mem, ffn1_pre_g, ffn1_w1, ffn1_w3, ffn1_w2, ffn1_post_g, mix_pre_g, w_in, b_gate, pool_w, pool_scale, w_pool_out, conv_w, conv_b, dt_bias, a_log, d_skip, ssd_norm_g, w_ssd_out, w_mix_out, mix_post_g, xattn_pre_g, mem_norm_g, w_q, w_k, w_v, w_xo, xattn_post_g, ffn2_pre_g, ffn2_w1, ffn2_w3, ffn2_w2, ffn2_post_g):
    batch, seq, d = x.shape
    depth = ffn1_w1.shape[0]
    c0 = D_MODEL
    c1 = c0 + SSD_INNER
    c2 = c1 + SSD_CONV_DIM
    c3 = c2 + SSD_HEADS
    vec = lambda g: g.reshape(1, -1).astype(F32)
    twice = lambda t: jnp.concatenate([t, t], axis=-1)

    h = x.reshape(batch * seq, d)
    mem2 = mem.reshape(batch * N_MEM, d)
    w_in_t = jnp.swapaxes(w_in, 1, 2)
    u = None
    for i in range(depth):
        ffn1 = (vec(ffn1_pre_g[i]), ffn1_w1[i], ffn1_w3[i], ffn1_w2[i], 0.5 * vec(ffn1_post_g[i]),
                vec(mix_pre_g[i]))
        h_head, u_head, ffn1_wb = _ffn(h, *ffn1, tf=FFN_HEAD_TF, rows=FFN_TM)
        h, u, _ = _ffn(h, ffn1[0], *ffn1_wb, *ffn1[4:], head=(h_head, u_head))

        p_in, wb_q, wb_k, wb_v, wb_xo, f2_w2 = _proj(
            u, w_in_t, 0, c0, "none", (), (w_q[i], w_k[i], w_v[i], w_xo[i], ffn2_w2[i]), seq=seq, layer=i)
        z_act, f2_w1 = _proj(u, w_in_t, c0, c1 - c0, "silu", (), (ffn2_w1[i],), seq=seq, layer=i,
                             out_dtype=BF16)
        xbc = _proj(u, w_in_t, c1, c2 - c1, "conv_silu", (conv_w[i], vec(conv_b[i])), seq=seq, layer=i)
        dt2 = _proj(u, w_in_t, c2, c3 - c2, "softplus_bias", (twice(vec(dt_bias[i])),), seq=seq, layer=i,
                    repeat=2)
        gates, wb_ssd_out, wb_mix_out, wb_pool_out, wb_pool, f2_w3 = _proj(
            u, w_in_t, c3, 2 * D_MODEL, "sigmoid_bias", (vec(b_gate[i]),),
            (w_ssd_out[i], w_mix_out[i], w_pool_out[i], pool_w[i].reshape(-1, POOL_GROUP_DIM), ffn2_w3[i]),
            seq=seq, layer=i, out_dtype=BF16)

        gaya = _pool(p_in, gates, wb_pool.reshape(pool_w.shape[1:]), vec(pool_scale[i]), wb_pool_out, seq)
        y = _ssd(xbc, dt2, twice(vec(a_log[i])), jnp.repeat(vec(d_skip[i]), SSD_HEAD_DIM, axis=1), batch)
        h, u = _merge(y, z_act, vec(ssd_norm_g[i]), wb_ssd_out, gaya, gates, wb_mix_out,
                      vec(mix_post_g[i]), h, vec(xattn_pre_g[i]))

        k, v = _kv(mem2, vec(mem_norm_g[i]), wb_k, wb_v)
        h = _attn(u, h, wb_q, k, v, wb_xo, vec(xattn_post_g[i]), seq)

        last = i == depth - 1
        h, u, _ = _ffn(h, vec(ffn2_pre_g[i]), f2_w1, f2_w3, f2_w2, 0.5 * vec(ffn2_post_g[i]),
                       None if last else vec(ffn1_pre_g[i + 1]))
    return h.reshape(batch, seq, d)
```

```python
import functools
import math

import jax
import jax.numpy as jnp
import numpy as np
from jax import lax
from jax.experimental import pallas as pl
from jax.experimental.pallas import tpu as pltpu

F32 = jnp.float32
BF16 = jnp.bfloat16

D_MODEL = 2048
D_FF = 5632
CHUNK = 64
N_MEM = 256
POOL_WINDOWS = (2, 4, 8, 16)
POOL_GROUP_DIM = D_MODEL // len(POOL_WINDOWS)
POOL_HALO = 16
SSD_INNER = 4096
SSD_HEAD_DIM = 64
SSD_HEADS = SSD_INNER // SSD_HEAD_DIM
SSD_GROUPS = 8
SSD_STATE = 128
SSD_GROUP_WIDTH = SSD_INNER // SSD_GROUPS
SSD_BC = SSD_GROUPS * SSD_STATE
SSD_CONV = 4
SSD_CONV_DIM = SSD_INNER + 2 * SSD_BC
X_HEADS = 4
X_HEAD_DIM = 128
X_WIDTH = X_HEADS * X_HEAD_DIM
EPS = 1e-6
LOG2_E = math.log2(math.e)

LANES = 128
F32_SUBLANES = 8
BF16_SUBLANES = 16
MXU_DIM = 256
V7X_VMEM_LIMIT = 60 * 1024 * 1024


def _params(semantics):
    return pltpu.CompilerParams(dimension_semantics=semantics, vmem_limit_bytes=V7X_VMEM_LIMIT)


def _rms(x, g):
    return x * lax.rsqrt(jnp.mean(x * x, axis=-1, keepdims=True) + EPS) * g


def _dot(a, b):
    return jnp.dot(a, b, preferred_element_type=F32)


def _dot_nt(a, b):
    return lax.dot_general(a, b, (((1,), (1,)), ((), ())), preferred_element_type=F32)


def _const_spec(shape):
    return pl.BlockSpec(shape, lambda *_: (0,) * len(shape), pipeline_mode=pl.Buffered(1))


def _ffn_kernel(x_ref, pre_ref, w1_ref, w3_ref, w2_ref, post_ref, *rest, emit_next, emit_weights, has_head,
                norm_ahead):
    rest = list(rest)
    next_ref = rest.pop(0) if emit_next else None
    head_refs = [rest.pop(0) for _ in range(2 if emit_next else 1)] if has_head else None
    xn_ref = rest.pop(0) if norm_ahead else None
    h_ref = rest.pop(0)
    un_ref = rest.pop(0) if emit_next else None
    wb_refs = [rest.pop(0) for _ in range(3)] if emit_weights else None
    u_sc, acc_sc = rest
    i, j = pl.program_id(0), pl.program_id(1)
    last = pl.num_programs(1) - 1
    first_tile = 1 if has_head else 0
    slot = i % 2 if norm_ahead else 0

    def hidden_step():
        w1, w3, w2 = w1_ref[...], w3_ref[...], w2_ref[...]
        if emit_weights:
            w1, w3, w2 = w1.astype(BF16), w3.astype(BF16), w2.astype(BF16)
            for dst, w in zip(wb_refs, (w1, w3, w2)):
                dst[...] = w
        u = u_sc[slot]
        a = _dot(u, w1)
        b = _dot(u, w3)
        act = (a * jax.nn.sigmoid(a) * b).astype(BF16)
        acc_sc[...] += _dot(act, w2)

    def finish():
        h = x_ref[...] + _rms(acc_sc[...], post_ref[...])
        h_ref[...] = h
        if emit_next:
            un_ref[...] = _rms(h, next_ref[...]).astype(BF16)

    def body():
        @pl.when(j == 0)
        def _():
            acc_sc[...] = jnp.zeros_like(acc_sc)

        @pl.when(jnp.logical_and(j == 0, jnp.logical_or(i == first_tile, not norm_ahead)))
        def _():
            u_sc[slot] = _rms(x_ref[...], pre_ref[...]).astype(BF16)

        if norm_ahead:
            pl.when(j < last)(hidden_step)

            @pl.when(j == last)
            def _():
                hidden_step()
                u_sc[1 - slot] = _rms(xn_ref[...], pre_ref[...]).astype(BF16)
                finish()
        else:
            hidden_step()
            pl.when(j == last)(finish)

    if has_head:
        @pl.when(jnp.logical_and(i == 0, j == last))
        def _():
            h_ref[...] = head_refs[0][...]
            if emit_next:
                un_ref[...] = head_refs[1][...]

        pl.when(i > 0)(body)
    else:
        body()


FFN_TM = 512
FFN_TF = 512
FFN_HEAD_TF = 256


def _ffn(x, pre_g, w1, w3, w2, post_g, next_g, *, tm=FFN_TM, tf=FFN_TF, rows=None, head=None):
    m, d = x.shape
    rows = m if rows is None else rows
    dff = w1.shape[1]
    emit_next = next_g is not None
    emit_weights = w1.dtype != BF16
    has_head = head is not None
    jw = (lambda i, j: jnp.where(i == 0, 0, j)) if has_head else (lambda i, j: j)
    row = pl.BlockSpec((tm, d), lambda i, j: (i, 0))
    vec = pl.BlockSpec((1, d), lambda i, j: (0, 0))
    w_up = pl.BlockSpec((d, tf), lambda i, j: (0, jw(i, j)))
    w_down = pl.BlockSpec((tf, d), lambda i, j: (jw(i, j), 0))
    in_specs = [row, vec, w_up, w_up, w_down, vec]
    args = [x, pre_g, w1, w3, w2, post_g]
    out_shape = [jax.ShapeDtypeStruct((rows, d), F32)]
    out_specs = [row]
    if emit_next:
        in_specs.append(vec)
        args.append(next_g)
        out_shape.append(jax.ShapeDtypeStruct((rows, d), BF16))
        out_specs.append(row)
    if has_head:
        for t in head[:2 if emit_next else 1]:
            in_specs.append(_const_spec(t.shape))
            args.append(t)
    n_tiles = rows // tm
    norm_ahead = n_tiles > 1
    if norm_ahead:
        in_specs.append(pl.BlockSpec((tm, d), lambda i, j: (jnp.minimum(i + 1, n_tiles - 1), 0)))
        args.append(x)
    if emit_weights:
        out_shape += [jax.ShapeDtypeStruct(w.shape, BF16) for w in (w1, w3, w2)]
        out_specs += [w_up, w_up, w_down]
    outs = pl.pallas_call(
        functools.partial(_ffn_kernel, emit_next=emit_next, emit_weights=emit_weights, has_head=has_head,
                          norm_ahead=norm_ahead),
        grid=(n_tiles, dff // tf),
        in_specs=in_specs,
        out_specs=out_specs,
        out_shape=out_shape,
        scratch_shapes=[pltpu.VMEM((2 if norm_ahead else 1, tm, d), BF16), pltpu.VMEM((tm, d), F32)],
        compiler_params=_params(("arbitrary" if norm_ahead else "parallel", "arbitrary")),
        name="ffn",
    )(*args)
    n_main = 2 if emit_next else 1
    return outs[0], (outs[1] if emit_next else None), (tuple(outs[n_main:]) if emit_weights else None)


def _softplus(x):
    return jnp.maximum(x, 0.0) + jnp.log1p(jnp.exp(-jnp.abs(x)))


CONV_CARRY = 8
PROJ_EXTRA_INPUTS = {"none": 0, "silu": 0, "sigmoid_bias": 1, "softplus_bias": 1, "conv_silu": 2}


def _proj_kernel(u_ref, w_ref, *rest, epilogue, tiles_per_seq, n_cast):
    rest = list(rest)
    extra = [rest.pop(0) for _ in range(PROJ_EXTRA_INPUTS[epilogue])]
    cast_in = [rest.pop(0) for _ in range(n_cast)]
    o_ref = rest.pop(0)
    cast_out = [rest.pop(0) for _ in range(n_cast)]
    w_sc = rest.pop(0)
    i = pl.program_id(1)

    @pl.when(i == 0)
    def _():
        w = w_ref[...].reshape(w_ref.shape[-2:]).astype(BF16)
        rows = w.shape[0]
        for r in range(w_sc.shape[0] // rows):
            w_sc[r * rows:(r + 1) * rows, :] = w

    for src, dst in zip(cast_in, cast_out):
        dst[...] = src[...].astype(BF16)

    acc = _dot_nt(u_ref[...], w_sc[...])
    if epilogue == "silu":
        acc = acc * jax.nn.sigmoid(acc)
    elif epilogue == "sigmoid_bias":
        acc = jax.nn.sigmoid(acc + extra[0][...])
    elif epilogue == "softplus_bias":
        acc = _softplus(acc + extra[0][...])
    elif epilogue == "conv_silu":
        convw_ref, convb_ref = extra
        carry_sc = rest.pop(0)
        tm = acc.shape[0]
        carry = jnp.where(i % tiles_per_seq == 0, 0.0, carry_sc[...])
        carry_sc[...] = acc[tm - CONV_CARRY:]
        assert SSD_CONV == 4
        ext = jnp.concatenate([carry, acc], axis=0)
        ext1 = pltpu.roll(ext, 1, axis=0)
        far = pltpu.roll(convw_ref[1:2, :] * ext + convw_ref[0:1, :] * ext1, 2, axis=0)
        conv = (convb_ref[...] + convw_ref[3:4, :] * acc + convw_ref[2:3, :] * ext1[CONV_CARRY:]
                + far[CONV_CARRY:])
        acc = conv * jax.nn.sigmoid(conv)
    o_ref[...] = acc.astype(o_ref.dtype)


def _proj(u, wt, col_start, n, epilogue, extra=(), to_cast=(), *, seq, layer, tm=1024, tn=1024, repeat=1,
          out_dtype=F32):
    m, k = u.shape
    tn = min(tn, n)
    assert n % tn == 0 and m % tm == 0 and seq % tm == 0 and (repeat == 1 or n == tn)
    steps_j, steps_i = n // tn, m // tm
    if col_start % tn == 0:
        col_block = col_start // tn
        w_spec = pl.BlockSpec((None, tn, k), lambda j, i: (layer, col_block + j, 0))
    else:
        assert col_start % F32_SUBLANES == 0 and tn % F32_SUBLANES == 0
        w_spec = pl.BlockSpec((pl.Element(1), pl.Element(tn), pl.Element(k)),
                              lambda j, i: (layer, pl.multiple_of(col_start + j * tn, F32_SUBLANES), 0))
    to = tn * repeat
    in_specs = [pl.BlockSpec((tm, k), lambda j, i: (i, 0)), w_spec]
    for e in extra:
        in_specs.append(pl.BlockSpec((e.shape[0], to), lambda j, i: (0, j)))
    out_specs = [pl.BlockSpec((tm, to), lambda j, i: (i, j))]
    out_shape = [jax.ShapeDtypeStruct((m, n * repeat), out_dtype)]
    for arr in to_cast:
        rows = arr.shape[0] // (steps_j * steps_i)
        assert rows * steps_j * steps_i == arr.shape[0] and rows % BF16_SUBLANES == 0, arr.shape
        spec = pl.BlockSpec((rows, arr.shape[1]), lambda j, i: (j * steps_i + i, 0))
        in_specs.append(spec)
        out_specs.append(spec)
        out_shape.append(jax.ShapeDtypeStruct(arr.shape, BF16))
    scratch = [pltpu.VMEM((to, k), BF16)]
    if epilogue == "conv_silu":
        scratch.append(pltpu.VMEM((CONV_CARRY, to), F32))
    outs = pl.pallas_call(
        functools.partial(_proj_kernel, epilogue=epilogue, tiles_per_seq=seq // tm, n_cast=len(to_cast)),
        grid=(steps_j, steps_i),
        in_specs=in_specs,
        out_specs=out_specs,
        out_shape=out_shape,
        scratch_shapes=scratch,
        compiler_params=_params(("parallel", "arbitrary")),
        name="proj_" + epilogue,
    )(u, wt, *extra, *to_cast)
    return outs if to_cast else outs[0]


def _pool_kernel(p_ref, halo_ref, ga_ref, pw_ref, scale_ref, wo_ref, o_ref, mixed_sc, *, tiles_per_seq):
    tm = p_ref.shape[0]
    i = pl.program_id(0)
    tile_in_seq = i % tiles_per_seq
    halo = jnp.where(tile_in_seq == 0, 0.0, halo_ref[...])
    pos = tile_in_seq * tm + lax.broadcasted_iota(jnp.int32, (tm, 1), 0)
    group_cols = [slice(g * POOL_GROUP_DIM, (g + 1) * POOL_GROUP_DIM) for g in range(len(POOL_WINDOWS))]
    pooled = []
    for cols, window in zip(group_cols, POOL_WINDOWS):
        x = p_ref[:, cols]
        s = jnp.concatenate([halo[:, cols], x], axis=0)
        width = 1
        while width < window:
            s = s[width:] + s[:-width]
            width *= 2
        win = s[POOL_HALO - window + 1:]
        inv_cnt = 1.0 / jnp.minimum(pos + 1, window).astype(F32)
        pooled.append((win * inv_cnt - x).astype(BF16))
    mixed = [_dot(pooled[g], pw_ref[g]) for g in range(len(POOL_WINDOWS))]
    for g, cols in enumerate(group_cols):
        mixed_sc[:, cols] = (mixed[g] * scale_ref[:, cols]).astype(BF16)
    o_ref[...] = (ga_ref[...].astype(F32) * _dot(mixed_sc[...], wo_ref[...])).astype(o_ref.dtype)


def _pool(p, gates, pool_w, pool_scale, w_out, seq, *, tm=512):
    m, d = p.shape
    halo_blocks = tm // POOL_HALO
    row = pl.BlockSpec((tm, d), lambda i: (i, 0))
    return pl.pallas_call(
        functools.partial(_pool_kernel, tiles_per_seq=seq // tm),
        grid=(m // tm,),
        in_specs=[row,
                  pl.BlockSpec((POOL_HALO, d), lambda i: (jnp.maximum(i * halo_blocks - 1, 0), 0)),
                  row,
                  _const_spec(pool_w.shape),
                  _const_spec((1, d)),
                  _const_spec(w_out.shape)],
        out_specs=row,
        out_shape=jax.ShapeDtypeStruct((m, d), BF16),
        scratch_shapes=[pltpu.VMEM((tm, d), BF16)],
        compiler_params=_params(("parallel",)),
        name="pool",
    )(p, p, gates, pool_w, pool_scale, w_out)


HEADS_PER_DOT = MXU_DIM // SSD_HEAD_DIM
SSD_CHUNKS_PER_STEP = 4


def _ssd_constants():
    lane = np.arange(SSD_INNER)
    head_of_lane = lane // SSD_HEAD_DIM
    pos_of_lane = lane % SSD_HEAD_DIM
    expand = np.zeros((MXU_DIM, SSD_INNER), np.float32)
    for piece in range(3):
        expand[piece * SSD_HEADS + head_of_lane, lane] = 1.0
    row = np.arange(CHUNK)[:, None]
    diag = (row == pos_of_lane[None, :]).astype(np.float32)
    causal = np.where(row >= pos_of_lane[None, :], 0.0, -np.inf).astype(np.float32)
    blk = np.arange(MXU_DIM) // SSD_HEAD_DIM
    blockdiag = (blk[:, None] == blk[None, :]).astype(np.float32)
    return (jnp.asarray(expand, BF16), jnp.asarray(diag), jnp.asarray(causal), jnp.asarray(blockdiag, BF16))


def _split_select(v, lower_half, pieces):
    p0 = v.astype(BF16).astype(F32)
    r1 = v - p0
    p1 = r1.astype(BF16).astype(F32)
    first = jnp.where(lower_half, p0, p1).astype(BF16)
    if pieces == 2:
        return first
    p2 = (r1 - p1).astype(BF16)
    return jnp.concatenate([first, p2], axis=1)


def _ssd_kernel(xbc_ref, dt_ref, alog_ref, dskip_ref, expand_ref, diag_ref, causal_ref, bd_ref, y_ref,
                state_sc, colb_sc, dtx_sc):
    @pl.when(pl.program_id(1) == 0)
    def _():
        state_sc[...] = jnp.zeros_like(state_sc)

    dt = dt_ref[...]
    a = dt * (-jnp.exp(alog_ref[...]) * LOG2_E)
    row_in_chunk = lax.broadcasted_iota(jnp.int32, a.shape, 0) % CHUNK
    lower_half = lax.broadcasted_iota(jnp.int32, a.shape, 1) < SSD_HEADS
    acs = a
    shift = 1
    while shift < CHUNK:
        acs = acs + jnp.where(row_in_chunk >= shift, pltpu.roll(acs, shift, axis=0), 0.0)
        shift *= 2

    colb_sc[...] = _dot(_split_select(acs, lower_half, 3), expand_ref[...])
    dtx_sc[...] = _dot(_split_select(dt, lower_half, 2), expand_ref[0:LANES, :])

    groups = range(SSD_GROUPS)
    group_cols = [slice(g * SSD_GROUP_WIDTH, (g + 1) * SSD_GROUP_WIDTH) for g in groups]
    parts = range(SSD_GROUP_WIDTH // MXU_DIM)
    chunk_rows = [slice(q * CHUNK, (q + 1) * CHUNK) for q in range(SSD_CHUNKS_PER_STEP)]

    def scores(q):
        rows = chunk_rows[q]
        b_g = [xbc_ref[rows, SSD_INNER + g * SSD_STATE:SSD_INNER + (g + 1) * SSD_STATE].astype(BF16)
               for g in groups]
        c_g = [xbc_ref[rows, SSD_INNER + SSD_BC + g * SSD_STATE:SSD_INNER + SSD_BC + (g + 1) * SSD_STATE]
               .astype(BF16) for g in groups]
        cb2 = [_dot_nt(c_g[g], jnp.concatenate([b_g[g], b_g[g]], axis=0)) for g in groups]
        return b_g, c_g, cb2

    def decays(q, cb2):
        rows = chunk_rows[q]
        gmat, x_bd, xend_b = [], [], []
        for g in groups:
            cols = group_cols[g]
            colb = colb_sc[rows, cols]
            rowb = jnp.sum(colb * diag_ref[:, cols], axis=0, keepdims=True)
            decay_ls = jnp.exp2(colb - rowb + causal_ref[:, cols])
            xdt = xbc_ref[rows, cols] * dtx_sc[rows, cols]
            xdt_b = xdt.astype(BF16)
            xend_b.append((xdt * jnp.exp2(colb[CHUNK - 1:CHUNK, :] - colb)).astype(BF16))
            cb = jnp.concatenate([cb2[g]] * (SSD_GROUP_WIDTH // LANES), axis=1)
            gmat.append((cb * decay_ls).astype(BF16))
            x_bd.append([jnp.concatenate([xdt_b[:, part * MXU_DIM:(part + 1) * MXU_DIM]] * HEADS_PER_DOT, axis=0)
                         * bd_ref[...] for part in parts])
        return gmat, x_bd, xend_b

    def products(b_g, c_g, gmat, x_bd, xend_b):
        y_prev = [_dot(c_g[g], state_sc[g].astype(BF16)) for g in groups]
        y_diag = [jnp.concatenate([_dot(gmat[g][:, part * MXU_DIM:(part + 1) * MXU_DIM], x_bd[g][part])
                                   for part in parts], axis=1) for g in groups]
        new_state = [lax.dot_general(b_g[g], xend_b[g], (((0,), (0,)), ((), ())), preferred_element_type=F32)
                     for g in groups]
        return y_prev, y_diag, new_state

    def finish(q, y_prev, y_diag, new_state):
        rows = chunk_rows[q]
        for g in groups:
            cols = group_cols[g]
            colb = colb_sc[rows, cols]
            y_ref[rows, cols] = (y_prev[g] * jnp.exp2(colb) + dskip_ref[:, cols] * xbc_ref[rows, cols]
                                 + y_diag[g]).astype(y_ref.dtype)
            state_sc[g] = state_sc[g] * jnp.exp2(colb[CHUNK - 1:CHUNK, :]) + new_state[g]

    for q in range(SSD_CHUNKS_PER_STEP):
        b_g, c_g, cb2 = scores(q)
        finish(q, *products(b_g, c_g, *decays(q, cb2)))


def _ssd(xbc, dt2, a_log2, dskip_x, batch):
    m = xbc.shape[0]
    rows = SSD_CHUNKS_PER_STEP * CHUNK
    steps = m // batch // rows
    assert steps * rows * batch == m
    expand, diag, causal, blockdiag = _ssd_constants()
    tok = lambda width: pl.BlockSpec((rows, width), lambda b, c: (b * steps + c, 0))
    return pl.pallas_call(
        _ssd_kernel,
        grid=(batch, steps),
        in_specs=[tok(SSD_CONV_DIM), tok(LANES), _const_spec(a_log2.shape),
                  _const_spec(dskip_x.shape), _const_spec(expand.shape), _const_spec(diag.shape),
                  _const_spec(causal.shape), _const_spec(blockdiag.shape)],
        out_specs=tok(SSD_INNER),
        out_shape=jax.ShapeDtypeStruct((m, SSD_INNER), BF16),
        scratch_shapes=[pltpu.VMEM((SSD_GROUPS, SSD_STATE, SSD_GROUP_WIDTH), F32),
                        pltpu.VMEM((rows, SSD_INNER), F32), pltpu.VMEM((rows, SSD_INNER), F32)],
        compiler_params=_params(("arbitrary", "arbitrary")),
        name="ssd",
    )(xbc, dt2, a_log2, dskip_x, expand, diag, causal, blockdiag)


def _merge_kernel(y_ref, z_ref, ng_ref, wso_ref, gaya_ref, gb_ref, wmix_ref, post_ref, h_ref, next_ref,
                  ho_ref, un_ref, yn_sc):
    yz = y_ref[...].astype(F32) * z_ref[...].astype(F32)
    for g in range(SSD_GROUPS):
        cols = slice(g * SSD_GROUP_WIDTH, (g + 1) * SSD_GROUP_WIDTH)
        yn_sc[:, cols] = _rms(yz[:, cols], ng_ref[:, cols]).astype(BF16)
    y_b = _dot(yn_sc[...], wso_ref[...])
    merged = (gaya_ref[...].astype(F32) + gb_ref[...].astype(F32) * y_b).astype(BF16)
    h = h_ref[...] + _rms(_dot(merged, wmix_ref[...]), post_ref[...])
    ho_ref[...] = h
    un_ref[...] = _rms(h, next_ref[...]).astype(BF16)


def _merge(y, z, norm_g, w_ssd_out, gaya, gates, w_mix, post_g, h, next_g, *, tm=256):
    m, d = h.shape
    wide = pl.BlockSpec((tm, SSD_INNER), lambda i: (i, 0))
    row = pl.BlockSpec((tm, d), lambda i: (i, 0))
    return pl.pallas_call(
        _merge_kernel,
        grid=(m // tm,),
        in_specs=[wide, wide, _const_spec((1, SSD_INNER)), _const_spec(w_ssd_out.shape),
                  row,
                  pl.BlockSpec((tm, d), lambda i: (i, 1)),
                  _const_spec(w_mix.shape), _const_spec((1, d)), row, _const_spec((1, d))],
        out_specs=[row, row],
        out_shape=[jax.ShapeDtypeStruct((m, d), F32), jax.ShapeDtypeStruct((m, d), BF16)],
        scratch_shapes=[pltpu.VMEM((tm, SSD_INNER), BF16)],
        compiler_params=_params(("parallel",)),
        name="merge",
    )(y, z, norm_g, w_ssd_out, gaya, gates, w_mix, post_g, h, next_g)


def _kv_kernel(mem_ref, g_ref, wk_ref, wv_ref, k_ref, v_ref):
    mem_n = _rms(mem_ref[...], g_ref[...]).astype(BF16)
    k_ref[...] = _dot(mem_n, wk_ref[...]).astype(BF16)
    v_ref[...] = _dot(mem_n, wv_ref[...]).astype(BF16)


def _kv(mem, g, w_k, w_v):
    rows = mem.shape[0]
    out = jax.ShapeDtypeStruct((rows, X_WIDTH), BF16)
    return pl.pallas_call(_kv_kernel, out_shape=[out, out], name="kv",
                          compiler_params=pltpu.CompilerParams(vmem_limit_bytes=V7X_VMEM_LIMIT))(mem, g, w_k, w_v)


def _attn_kernel(u_ref, h_ref, wq_ref, k_ref, v_ref, wo_ref, post_ref, ho_ref, o_sc):
    q = _dot(u_ref[...], wq_ref[...]).astype(BF16)
    heads = range(X_HEADS)
    head_cols = [slice(hd * X_HEAD_DIM, (hd + 1) * X_HEAD_DIM) for hd in heads]
    scores = [_dot_nt(q[:, cols], k_ref[:, cols]) * (1.0 / math.sqrt(X_HEAD_DIM)) for cols in head_cols]
    weights = [jnp.exp(s - jnp.max(s, axis=-1, keepdims=True)) for s in scores]
    denom = [jnp.sum(e, axis=-1, keepdims=True) for e in weights]
    mixed = [_dot(weights[hd].astype(BF16), v_ref[:, head_cols[hd]]) for hd in heads]
    for hd in heads:
        o_sc[:, head_cols[hd]] = (mixed[hd] / denom[hd]).astype(BF16)
    ho_ref[...] = h_ref[...] + _rms(_dot(o_sc[...], wo_ref[...]), post_ref[...])


def _attn(u, h, w_q, k, v, w_o, post_g, seq, *, tm=512):
    m, d = h.shape
    tiles_per_seq = seq // tm
    row = pl.BlockSpec((tm, d), lambda i: (i, 0))
    mem = pl.BlockSpec((N_MEM, X_WIDTH), lambda i: (i // tiles_per_seq, 0))
    return pl.pallas_call(
        _attn_kernel,
        grid=(m // tm,),
        in_specs=[row, row, _const_spec(w_q.shape), mem, mem, _const_spec(w_o.shape),
                  _const_spec((1, d))],
        out_specs=row,
        out_shape=jax.ShapeDtypeStruct((m, d), F32),
        scratch_shapes=[pltpu.VMEM((tm, X_WIDTH), BF16)],
        compiler_params=_params(("parallel",)),
        name="attn",
    )(u, h, w_q, k, v, w_o, post_g)


def kernel(x, mem, ffn1_pre_g, ffn1_w1, ffn1_w3, ffn1_w2, ffn1_post_g, mix_pre_g, w_in, b_gate, pool_w, pool_scale, w_pool_out, conv_w, conv_b, dt_bias, a_log, d_skip, ssd_norm_g, w_ssd_out, w_mix_out, mix_post_g, xattn_pre_g, mem_norm_g, w_q, w_k, w_v, w_xo, xattn_post_g, ffn2_pre_g, ffn2_w1, ffn2_w3, ffn2_w2, ffn2_post_g):
    batch, seq, d = x.shape
    depth = ffn1_w1.shape[0]
    c0 = D_MODEL
    c1 = c0 + SSD_INNER
    c2 = c1 + SSD_CONV_DIM
    c3 = c2 + SSD_HEADS
    vec = lambda g: g.reshape(1, -1).astype(F32)
    twice = lambda t: jnp.concatenate([t, t], axis=-1)

    h = x.reshape(batch * seq, d)
    mem2 = mem.reshape(batch * N_MEM, d)
    w_in_t = jnp.swapaxes(w_in, 1, 2)
    u = None
    for i in range(depth):
        ffn1 = (vec(ffn1_pre_g[i]), ffn1_w1[i], ffn1_w3[i], ffn1_w2[i], 0.5 * vec(ffn1_post_g[i]),
                vec(mix_pre_g[i]))
        h_head, u_head, ffn1_wb = _ffn(h, *ffn1, tf=FFN_HEAD_TF, rows=FFN_TM)
        h, u, _ = _ffn(h, ffn1[0], *ffn1_wb, *ffn1[4:], head=(h_head, u_head))

        p_in, wb_q, wb_k, wb_v, wb_xo, f2_w2 = _proj(
            u, w_in_t, 0, c0, "none", (), (w_q[i], w_k[i], w_v[i], w_xo[i], ffn2_w2[i]), seq=seq, layer=i)
        z_act, f2_w1 = _proj(u, w_in_t, c0, c1 - c0, "silu", (), (ffn2_w1[i],), seq=seq, layer=i,
                             out_dtype=BF16)
        xbc = _proj(u, w_in_t, c1, c2 - c1, "conv_silu", (conv_w[i], vec(conv_b[i])), seq=seq, layer=i)
        dt2 = _proj(u, w_in_t, c2, c3 - c2, "softplus_bias", (twice(vec(dt_bias[i])),), seq=seq, layer=i,
                    repeat=2)
        gates, wb_ssd_out, wb_mix_out, wb_pool_out, wb_pool, f2_w3 = _proj(
            u, w_in_t, c3, 2 * D_MODEL, "sigmoid_bias", (vec(b_gate[i]),),
            (w_ssd_out[i], w_mix_out[i], w_pool_out[i], pool_w[i].reshape(-1, POOL_GROUP_DIM), ffn2_w3[i]),
            seq=seq, layer=i, out_dtype=BF16)

        gaya = _pool(p_in, gates, wb_pool.reshape(pool_w.shape[1:]), vec(pool_scale[i]), wb_pool_out, seq)
        y = _ssd(xbc, dt2, twice(vec(a_log[i])), jnp.repeat(vec(d_skip[i]), SSD_HEAD_DIM, axis=1), batch)
        h, u = _merge(y, z_act, vec(ssd_norm_g[i]), wb_ssd_out, gaya, gates, wb_mix_out,
                      vec(mix_post_g[i]), h, vec(xattn_pre_g[i]))

        k, v = _kv(mem2, vec(mem_norm_g[i]), wb_k, wb_v)
        h = _attn(u, h, wb_q, k, v, wb_xo, vec(xattn_post_g[i]), seq)

        last = i == depth - 1
        h, u, _ = _ffn(h, vec(ffn2_pre_g[i]), f2_w1, f2_w3, f2_w2, 0.5 * vec(ffn2_post_g[i]),
                       None if last else vec(ffn1_pre_g[i + 1]))
    return h.reshape(batch, seq, d)
```

```python
import functools
import math

import jax
import jax.numpy as jnp
import numpy as np
from jax import lax
from jax.experimental import pallas as pl
from jax.experimental.pallas import tpu as pltpu

F32 = jnp.float32
BF16 = jnp.bfloat16

D_MODEL = 2048
CHUNK = 64
N_MEM = 256
POOL_WINDOWS = (2, 4, 8, 16)
POOL_GROUP_DIM = D_MODEL // len(POOL_WINDOWS)
POOL_HALO = 16
SSD_INNER = 4096
SSD_HEAD_DIM = 64
SSD_HEADS = SSD_INNER // SSD_HEAD_DIM
SSD_GROUPS = 8
SSD_STATE = 128
SSD_GROUP_WIDTH = SSD_INNER // SSD_GROUPS
SSD_BC = SSD_GROUPS * SSD_STATE
SSD_CONV = 4
SSD_CONV_DIM = SSD_INNER + 2 * SSD_BC
X_HEADS = 4
X_HEAD_DIM = 128
X_WIDTH = X_HEADS * X_HEAD_DIM
EPS = 1e-6
LOG2_E = math.log2(math.e)

LANES = 128
F32_SUBLANES = 8
BF16_SUBLANES = 16
MXU_DIM = 256
V7X_VMEM_LIMIT = 60 * 1024 * 1024


def _params(semantics):
    return pltpu.CompilerParams(dimension_semantics=semantics, vmem_limit_bytes=V7X_VMEM_LIMIT)


def _rms(x, g):
    return x * lax.rsqrt(jnp.mean(x * x, axis=-1, keepdims=True) + EPS) * g


def _dot(a, b):
    return jnp.dot(a, b, preferred_element_type=F32)


def _dot_nt(a, b):
    return lax.dot_general(a, b, (((1,), (1,)), ((), ())), preferred_element_type=F32)


def _const_spec(shape):
    return pl.BlockSpec(shape, lambda *_: (0,) * len(shape), pipeline_mode=pl.Buffered(1))


def _ffn_kernel(x_ref, pre_ref, w1_ref, w3_ref, w2_ref, post_ref, *rest, emit_next, emit_weights, has_head):
    rest = list(rest)
    next_ref = rest.pop(0) if emit_next else None
    head_refs = [rest.pop(0) for _ in range(2 if emit_next else 1)] if has_head else None
    h_ref = rest.pop(0)
    un_ref = rest.pop(0) if emit_next else None
    wb_refs = [rest.pop(0) for _ in range(3)] if emit_weights else None
    u_sc, acc_sc = rest
    i, j = pl.program_id(0), pl.program_id(1)
    last = pl.num_programs(1) - 1

    def hidden_step():
        w1, w3, w2 = w1_ref[...], w3_ref[...], w2_ref[...]
        if emit_weights:
            w1, w3, w2 = w1.astype(BF16), w3.astype(BF16), w2.astype(BF16)
            for dst, w in zip(wb_refs, (w1, w3, w2)):
                dst[...] = w
        u = u_sc[...]
        a = _dot(u, w1)
        b = _dot(u, w3)
        act = (a * jax.nn.sigmoid(a) * b).astype(BF16)
        acc_sc[...] += _dot(act, w2)

    def body():
        @pl.when(j == 0)
        def _():
            u_sc[...] = _rms(x_ref[...], pre_ref[...]).astype(BF16)
            acc_sc[...] = jnp.zeros_like(acc_sc)

        pl.when(j < last)(hidden_step)

        @pl.when(j == last)
        def _():
            hidden_step()
            h = x_ref[...] + _rms(acc_sc[...], post_ref[...])
            h_ref[...] = h
            if emit_next:
                un_ref[...] = _rms(h, next_ref[...]).astype(BF16)

    if has_head:
        @pl.when(jnp.logical_and(i == 0, j == last))
        def _():
            h_ref[...] = head_refs[0][...]
            if emit_next:
                un_ref[...] = head_refs[1][...]

        pl.when(i > 0)(body)
    else:
        body()


FFN_TM = 512
FFN_TF = 512
FFN_HEAD_TF = 256


def _ffn(x, pre_g, w1, w3, w2, post_g, next_g, *, tm=FFN_TM, tf=FFN_TF, rows=None, head=None):
    m, d = x.shape
    rows = m if rows is None else rows
    dff = w1.shape[1]
    emit_next = next_g is not None
    emit_weights = w1.dtype != BF16
    has_head = head is not None
    jw = (lambda i, j: jnp.where(i == 0, 0, j)) if has_head else (lambda i, j: j)
    row = pl.BlockSpec((tm, d), lambda i, j: (i, 0))
    vec = pl.BlockSpec((1, d), lambda i, j: (0, 0))
    w_up = pl.BlockSpec((d, tf), lambda i, j: (0, jw(i, j)))
    w_down = pl.BlockSpec((tf, d), lambda i, j: (jw(i, j), 0))
    in_specs = [row, vec, w_up, w_up, w_down, vec]
    args = [x, pre_g, w1, w3, w2, post_g]
    out_shape = [jax.ShapeDtypeStruct((rows, d), F32)]
    out_specs = [row]
    if emit_next:
        in_specs.append(vec)
        args.append(next_g)
        out_shape.append(jax.ShapeDtypeStruct((rows, d), BF16))
        out_specs.append(row)
    if has_head:
        for t in head[:2 if emit_next else 1]:
            in_specs.append(_const_spec(t.shape))
            args.append(t)
    if emit_weights:
        out_shape += [jax.ShapeDtypeStruct(w.shape, BF16) for w in (w1, w3, w2)]
        out_specs += [w_up, w_up, w_down]
    outs = pl.pallas_call(
        functools.partial(_ffn_kernel, emit_next=emit_next, emit_weights=emit_weights, has_head=has_head),
        grid=(rows // tm, dff // tf),
        in_specs=in_specs,
        out_specs=out_specs,
        out_shape=out_shape,
        scratch_shapes=[pltpu.VMEM((tm, d), BF16), pltpu.VMEM((tm, d), F32)],
        compiler_params=_params(("parallel", "arbitrary")),
        name="ffn",
    )(*args)
    n_main = 2 if emit_next else 1
    return outs[0], (outs[1] if emit_next else None), (tuple(outs[n_main:]) if emit_weights else None)


def _softplus(x):
    return jnp.maximum(x, 0.0) + jnp.log1p(jnp.exp(-jnp.abs(x)))


CONV_CARRY = 8
PROJ_EXTRA_INPUTS = {"silu": 0, "sigmoid_bias": 1, "softplus_bias": 1, "conv_silu": 2}


def _proj_kernel(u_ref, w_ref, *rest, epilogue, tiles_per_seq, n_cast):
    rest = list(rest)
    extra = [rest.pop(0) for _ in range(PROJ_EXTRA_INPUTS[epilogue])]
    cast_in = [rest.pop(0) for _ in range(n_cast)]
    o_ref = rest.pop(0)
    cast_out = [rest.pop(0) for _ in range(n_cast)]
    w_sc = rest.pop(0)
    i = pl.program_id(1)

    @pl.when(i == 0)
    def _():
        w = w_ref[...].reshape(w_ref.shape[-2:]).astype(BF16)
        rows = w.shape[0]
        for r in range(w_sc.shape[0] // rows):
            w_sc[r * rows:(r + 1) * rows, :] = w

    for src, dst in zip(cast_in, cast_out):
        dst[...] = src[...].astype(BF16)

    acc = _dot_nt(u_ref[...], w_sc[...])
    if epilogue == "silu":
        acc = acc * jax.nn.sigmoid(acc)
    elif epilogue == "sigmoid_bias":
        acc = jax.nn.sigmoid(acc + extra[0][...])
    elif epilogue == "softplus_bias":
        acc = _softplus(acc + extra[0][...])
    elif epilogue == "conv_silu":
        convw_ref, convb_ref = extra
        carry_sc = rest.pop(0)
        tm = acc.shape[0]
        carry = jnp.where(i % tiles_per_seq == 0, 0.0, carry_sc[...])
        carry_sc[...] = acc[tm - CONV_CARRY:]
        assert SSD_CONV == 4
        ext = jnp.concatenate([carry, acc], axis=0)
        ext1 = pltpu.roll(ext, 1, axis=0)
        far = pltpu.roll(convw_ref[1:2, :] * ext + convw_ref[0:1, :] * ext1, 2, axis=0)
        conv = (convb_ref[...] + convw_ref[3:4, :] * acc + convw_ref[2:3, :] * ext1[CONV_CARRY:]
                + far[CONV_CARRY:])
        acc = conv * jax.nn.sigmoid(conv)
    o_ref[...] = acc.astype(o_ref.dtype)


def _proj(u, wt, col_start, n, epilogue, extra=(), to_cast=(), *, seq, layer, tm=1024, tn=1024, repeat=1,
          out_dtype=F32):
    m, k = u.shape
    tn = min(tn, n)
    assert n % tn == 0 and m % tm == 0 and seq % tm == 0 and (repeat == 1 or n == tn)
    steps_j, steps_i = n // tn, m // tm
    if col_start % tn == 0:
        col_block = col_start // tn
        w_spec = pl.BlockSpec((None, tn, k), lambda j, i: (layer, col_block + j, 0))
    else:
        assert col_start % F32_SUBLANES == 0 and tn % F32_SUBLANES == 0
        w_spec = pl.BlockSpec((pl.Element(1), pl.Element(tn), pl.Element(k)),
                              lambda j, i: (layer, pl.multiple_of(col_start + j * tn, F32_SUBLANES), 0))
    to = tn * repeat
    in_specs = [pl.BlockSpec((tm, k), lambda j, i: (i, 0)), w_spec]
    for e in extra:
        in_specs.append(pl.BlockSpec((e.shape[0], to), lambda j, i: (0, j)))
    out_specs = [pl.BlockSpec((tm, to), lambda j, i: (i, j))]
    out_shape = [jax.ShapeDtypeStruct((m, n * repeat), out_dtype)]
    to_cast = [t if isinstance(t, tuple) else (t, t.shape[0]) for t in to_cast]
    for arr, n_rows in to_cast:
        rows = n_rows // (steps_j * steps_i)
        assert rows * steps_j * steps_i == n_rows and rows % BF16_SUBLANES == 0, (arr.shape, n_rows)
        spec = pl.BlockSpec((rows, arr.shape[1]), lambda j, i: (j * steps_i + i, 0))
        in_specs.append(spec)
        out_specs.append(spec)
        out_shape.append(jax.ShapeDtypeStruct((n_rows, arr.shape[1]), BF16))
    scratch = [pltpu.VMEM((to, k), BF16)]
    if epilogue == "conv_silu":
        scratch.append(pltpu.VMEM((CONV_CARRY, to), F32))
    outs = pl.pallas_call(
        functools.partial(_proj_kernel, epilogue=epilogue, tiles_per_seq=seq // tm, n_cast=len(to_cast)),
        grid=(steps_j, steps_i),
        in_specs=in_specs,
        out_specs=out_specs,
        out_shape=out_shape,
        scratch_shapes=scratch,
        compiler_params=_params(("parallel", "arbitrary")),
        name="proj_" + epilogue,
    )(u, wt, *extra, *[arr for arr, _ in to_cast])
    return outs if to_cast else outs[0]


def _pool_kernel(u_ref, wp_ref, ga_ref, pw_ref, scale_ref, wo_ref, o_ref, mixed_sc, halo_sc, *, tiles_per_seq):
    tm = u_ref.shape[0]
    i = pl.program_id(0)
    tile_in_seq = i % tiles_per_seq
    p = _dot_nt(u_ref[...], wp_ref[...])
    halo = jnp.where(tile_in_seq == 0, 0.0, halo_sc[...])
    halo_sc[...] = p[tm - POOL_HALO:]
    pos = tile_in_seq * tm + lax.broadcasted_iota(jnp.int32, (tm, 1), 0)
    group_cols = [slice(g * POOL_GROUP_DIM, (g + 1) * POOL_GROUP_DIM) for g in range(len(POOL_WINDOWS))]
    pooled = []
    for cols, window in zip(group_cols, POOL_WINDOWS):
        x = p[:, cols]
        s = jnp.concatenate([halo[:, cols], x], axis=0)
        width = 1
        while width < window:
            s = s[width:] + s[:-width]
            width *= 2
        win = s[POOL_HALO - window + 1:]
        inv_cnt = 1.0 / jnp.minimum(pos + 1, window).astype(F32)
        pooled.append((win * inv_cnt - x).astype(BF16))
    mixed = [_dot(pooled[g], pw_ref[g]) for g in range(len(POOL_WINDOWS))]
    for g, cols in enumerate(group_cols):
        mixed_sc[:, cols] = (mixed[g] * scale_ref[:, cols]).astype(BF16)
    o_ref[...] = (ga_ref[...].astype(F32) * _dot(mixed_sc[...], wo_ref[...])).astype(o_ref.dtype)


def _pool(u, wp_t, gates, pool_w, pool_scale, w_out, seq, *, tm=512):
    m, d = u.shape
    assert seq % tm == 0
    row = pl.BlockSpec((tm, d), lambda i: (i, 0))
    return pl.pallas_call(
        functools.partial(_pool_kernel, tiles_per_seq=seq // tm),
        grid=(m // tm,),
        in_specs=[row,
                  _const_spec(wp_t.shape),
                  row,
                  _const_spec(pool_w.shape),
                  _const_spec((1, d)),
                  _const_spec(w_out.shape)],
        out_specs=row,
        out_shape=jax.ShapeDtypeStruct((m, d), BF16),
        scratch_shapes=[pltpu.VMEM((tm, d), BF16), pltpu.VMEM((POOL_HALO, d), F32)],
        compiler_params=_params(("arbitrary",)),
        name="pool",
    )(u, wp_t, gates, pool_w, pool_scale, w_out)


HEADS_PER_DOT = MXU_DIM // SSD_HEAD_DIM
SSD_CHUNKS_PER_STEP = 4


def _ssd_constants():
    lane = np.arange(SSD_INNER)
    head_of_lane = lane // SSD_HEAD_DIM
    pos_of_lane = lane % SSD_HEAD_DIM
    expand = np.zeros((MXU_DIM, SSD_INNER), np.float32)
    for piece in range(3):
        expand[piece * SSD_HEADS + head_of_lane, lane] = 1.0
    row = np.arange(CHUNK)[:, None]
    diag = (row == pos_of_lane[None, :]).astype(np.float32)
    causal = np.where(row >= pos_of_lane[None, :], 0.0, -np.inf).astype(np.float32)
    blk = np.arange(MXU_DIM) // SSD_HEAD_DIM
    blockdiag = (blk[:, None] == blk[None, :]).astype(np.float32)
    return (jnp.asarray(expand, BF16), jnp.asarray(diag), jnp.asarray(causal), jnp.asarray(blockdiag, BF16))


def _split_select(v, lower_half, pieces):
    p0 = v.astype(BF16).astype(F32)
    r1 = v - p0
    p1 = r1.astype(BF16).astype(F32)
    first = jnp.where(lower_half, p0, p1).astype(BF16)
    if pieces == 2:
        return first
    p2 = (r1 - p1).astype(BF16)
    return jnp.concatenate([first, p2], axis=1)


def _ssd_kernel(xbc_ref, dt_ref, alog_ref, dskip_ref, expand_ref, diag_ref, causal_ref, bd_ref, y_ref,
                state_sc, colb_sc, dtx_sc):
    @pl.when(pl.program_id(1) == 0)
    def _():
        state_sc[...] = jnp.zeros_like(state_sc)

    dt = dt_ref[...]
    a = dt * (-jnp.exp(alog_ref[...]) * LOG2_E)
    row_in_chunk = lax.broadcasted_iota(jnp.int32, a.shape, 0) % CHUNK
    lower_half = lax.broadcasted_iota(jnp.int32, a.shape, 1) < SSD_HEADS
    acs = a
    shift = 1
    while shift < CHUNK:
        acs = acs + jnp.where(row_in_chunk >= shift, pltpu.roll(acs, shift, axis=0), 0.0)
        shift *= 2

    colb_sc[...] = _dot(_split_select(acs, lower_half, 3), expand_ref[...])
    dtx_sc[...] = _dot(_split_select(dt, lower_half, 2), expand_ref[0:LANES, :])

    groups = range(SSD_GROUPS)
    group_cols = [slice(g * SSD_GROUP_WIDTH, (g + 1) * SSD_GROUP_WIDTH) for g in groups]
    parts = range(SSD_GROUP_WIDTH // MXU_DIM)
    chunk_rows = [slice(q * CHUNK, (q + 1) * CHUNK) for q in range(SSD_CHUNKS_PER_STEP)]

    def scores(q):
        rows = chunk_rows[q]
        b_g = [xbc_ref[rows, SSD_INNER + g * SSD_STATE:SSD_INNER + (g + 1) * SSD_STATE].astype(BF16)
               for g in groups]
        c_g = [xbc_ref[rows, SSD_INNER + SSD_BC + g * SSD_STATE:SSD_INNER + SSD_BC + (g + 1) * SSD_STATE]
               .astype(BF16) for g in groups]
        cb2 = [_dot_nt(c_g[g], jnp.concatenate([b_g[g], b_g[g]], axis=0)) for g in groups]
        return b_g, c_g, cb2

    def decays(q, cb2):
        rows = chunk_rows[q]
        gmat, x_bd, xend_b = [], [], []
        for g in groups:
            cols = group_cols[g]
            colb = colb_sc[rows, cols]
            rowb = jnp.sum(colb * diag_ref[:, cols], axis=0, keepdims=True)
            decay_ls = jnp.exp2(colb - rowb + causal_ref[:, cols])
            xdt = xbc_ref[rows, cols] * dtx_sc[rows, cols]
            xdt_b = xdt.astype(BF16)
            xend_b.append((xdt * jnp.exp2(colb[CHUNK - 1:CHUNK, :] - colb)).astype(BF16))
            cb = jnp.concatenate([cb2[g]] * (SSD_GROUP_WIDTH // LANES), axis=1)
            gmat.append((cb * decay_ls).astype(BF16))
            x_bd.append([jnp.concatenate([xdt_b[:, part * MXU_DIM:(part + 1) * MXU_DIM]] * HEADS_PER_DOT, axis=0)
                         * bd_ref[...] for part in parts])
        return gmat, x_bd, xend_b

    def products(b_g, c_g, gmat, x_bd, xend_b):
        y_prev = [_dot(c_g[g], state_sc[g].astype(BF16)) for g in groups]
        y_diag = [jnp.concatenate([_dot(gmat[g][:, part * MXU_DIM:(part + 1) * MXU_DIM], x_bd[g][part])
                                   for part in parts], axis=1) for g in groups]
        new_state = [lax.dot_general(b_g[g], xend_b[g], (((0,), (0,)), ((), ())), preferred_element_type=F32)
                     for g in groups]
        return y_prev, y_diag, new_state

    def finish(q, y_prev, y_diag, new_state):
        rows = chunk_rows[q]
        for g in groups:
            cols = group_cols[g]
            colb = colb_sc[rows, cols]
            y_ref[rows, cols] = (y_prev[g] * jnp.exp2(colb) + dskip_ref[:, cols] * xbc_ref[rows, cols]
                                 + y_diag[g]).astype(y_ref.dtype)
            state_sc[g] = state_sc[g] * jnp.exp2(colb[CHUNK - 1:CHUNK, :]) + new_state[g]

    for q in range(SSD_CHUNKS_PER_STEP):
        b_g, c_g, cb2 = scores(q)
        finish(q, *products(b_g, c_g, *decays(q, cb2)))


def _ssd(xbc, dt2, a_log2, dskip_x, batch):
    m = xbc.shape[0]
    rows = SSD_CHUNKS_PER_STEP * CHUNK
    steps = m // batch // rows
    assert steps * rows * batch == m
    expand, diag, causal, blockdiag = _ssd_constants()
    tok = lambda width: pl.BlockSpec((rows, width), lambda b, c: (b * steps + c, 0))
    return pl.pallas_call(
        _ssd_kernel,
        grid=(batch, steps),
        in_specs=[tok(SSD_CONV_DIM), tok(LANES), _const_spec(a_log2.shape),
                  _const_spec(dskip_x.shape), _const_spec(expand.shape), _const_spec(diag.shape),
                  _const_spec(causal.shape), _const_spec(blockdiag.shape)],
        out_specs=tok(SSD_INNER),
        out_shape=jax.ShapeDtypeStruct((m, SSD_INNER), BF16),
        scratch_shapes=[pltpu.VMEM((SSD_GROUPS, SSD_STATE, SSD_GROUP_WIDTH), F32),
                        pltpu.VMEM((rows, SSD_INNER), F32), pltpu.VMEM((rows, SSD_INNER), F32)],
        compiler_params=_params(("arbitrary", "arbitrary")),
        name="ssd",
    )(xbc, dt2, a_log2, dskip_x, expand, diag, causal, blockdiag)


def _merge_kernel(y_ref, z_ref, ng_ref, wso_ref, gaya_ref, gb_ref, wmix_ref, post_ref, h_ref, next_ref,
                  ho_ref, un_ref, yn_sc):
    yz = y_ref[...].astype(F32) * z_ref[...].astype(F32)
    for g in range(SSD_GROUPS):
        cols = slice(g * SSD_GROUP_WIDTH, (g + 1) * SSD_GROUP_WIDTH)
        yn_sc[:, cols] = _rms(yz[:, cols], ng_ref[:, cols]).astype(BF16)
    y_b = _dot(yn_sc[...], wso_ref[...])
    merged = (gaya_ref[...].astype(F32) + gb_ref[...].astype(F32) * y_b).astype(BF16)
    h = h_ref[...] + _rms(_dot(merged, wmix_ref[...]), post_ref[...])
    ho_ref[...] = h
    un_ref[...] = _rms(h, next_ref[...]).astype(BF16)


def _merge(y, z, norm_g, w_ssd_out, gaya, gates, w_mix, post_g, h, next_g, *, tm=256):
    m, d = h.shape
    wide = pl.BlockSpec((tm, SSD_INNER), lambda i: (i, 0))
    row = pl.BlockSpec((tm, d), lambda i: (i, 0))
    return pl.pallas_call(
        _merge_kernel,
        grid=(m // tm,),
        in_specs=[wide, wide, _const_spec((1, SSD_INNER)), _const_spec(w_ssd_out.shape),
                  row,
                  pl.BlockSpec((tm, d), lambda i: (i, 1)),
                  _const_spec(w_mix.shape), _const_spec((1, d)), row, _const_spec((1, d))],
        out_specs=[row, row],
        out_shape=[jax.ShapeDtypeStruct((m, d), F32), jax.ShapeDtypeStruct((m, d), BF16)],
        scratch_shapes=[pltpu.VMEM((tm, SSD_INNER), BF16)],
        compiler_params=_params(("parallel",)),
        name="merge",
    )(y, z, norm_g, w_ssd_out, gaya, gates, w_mix, post_g, h, next_g)


def _kv_kernel(mem_ref, g_ref, wk_ref, wv_ref, k_ref, v_ref):
    mem_n = _rms(mem_ref[...], g_ref[...]).astype(BF16)
    k_ref[...] = _dot(mem_n, wk_ref[...]).astype(BF16)
    v_ref[...] = _dot(mem_n, wv_ref[...]).astype(BF16)


def _kv(mem, g, w_k, w_v):
    rows = mem.shape[0]
    out = jax.ShapeDtypeStruct((rows, X_WIDTH), BF16)
    return pl.pallas_call(_kv_kernel, out_shape=[out, out], name="kv",
                          compiler_params=pltpu.CompilerParams(vmem_limit_bytes=V7X_VMEM_LIMIT))(mem, g, w_k, w_v)


def _attn_kernel(u_ref, h_ref, wq_ref, k_ref, v_ref, wo_ref, post_ref, ho_ref, o_sc):
    q = _dot(u_ref[...], wq_ref[...]).astype(BF16)
    heads = range(X_HEADS)
    head_cols = [slice(hd * X_HEAD_DIM, (hd + 1) * X_HEAD_DIM) for hd in heads]
    scores = [_dot_nt(q[:, cols], k_ref[:, cols]) * (1.0 / math.sqrt(X_HEAD_DIM)) for cols in head_cols]
    weights = [jnp.exp(s - jnp.max(s, axis=-1, keepdims=True)) for s in scores]
    denom = [jnp.sum(e, axis=-1, keepdims=True) for e in weights]
    mixed = [_dot(weights[hd].astype(BF16), v_ref[:, head_cols[hd]]) for hd in heads]
    for hd in heads:
        o_sc[:, head_cols[hd]] = (mixed[hd] / denom[hd]).astype(BF16)
    ho_ref[...] = h_ref[...] + _rms(_dot(o_sc[...], wo_ref[...]), post_ref[...])


def _attn(u, h, w_q, k, v, w_o, post_g, seq, *, tm=512):
    m, d = h.shape
    tiles_per_seq = seq // tm
    row = pl.BlockSpec((tm, d), lambda i: (i, 0))
    mem = pl.BlockSpec((N_MEM, X_WIDTH), lambda i: (i // tiles_per_seq, 0))
    return pl.pallas_call(
        _attn_kernel,
        grid=(m // tm,),
        in_specs=[row, row, _const_spec(w_q.shape), mem, mem, _const_spec(w_o.shape),
                  _const_spec((1, d))],
        out_specs=row,
        out_shape=jax.ShapeDtypeStruct((m, d), F32),
        scratch_shapes=[pltpu.VMEM((tm, X_WIDTH), BF16)],
        compiler_params=_params(("parallel",)),
        name="attn",
    )(u, h, w_q, k, v, w_o, post_g)


def kernel(x, mem, ffn1_pre_g, ffn1_w1, ffn1_w3, ffn1_w2, ffn1_post_g, mix_pre_g, w_in, b_gate, pool_w, pool_scale, w_pool_out, conv_w, conv_b, dt_bias, a_log, d_skip, ssd_norm_g, w_ssd_out, w_mix_out, mix_post_g, xattn_pre_g, mem_norm_g, w_q, w_k, w_v, w_xo, xattn_post_g, ffn2_pre_g, ffn2_w1, ffn2_w3, ffn2_w2, ffn2_post_g):
    batch, seq, d = x.shape
    depth = ffn1_w1.shape[0]
    c0 = D_MODEL
    c1 = c0 + SSD_INNER
    c2 = c1 + SSD_CONV_DIM
    c3 = c2 + SSD_HEADS
    vec = lambda g: g.reshape(1, -1).astype(F32)
    twice = lambda t: jnp.concatenate([t, t], axis=-1)

    h = x.reshape(batch * seq, d)
    mem2 = mem.reshape(batch * N_MEM, d)
    w_in_t = jnp.swapaxes(w_in, 1, 2)
    u = None
    for i in range(depth):
        ffn1 = (vec(ffn1_pre_g[i]), ffn1_w1[i], ffn1_w3[i], ffn1_w2[i], 0.5 * vec(ffn1_post_g[i]),
                vec(mix_pre_g[i]))
        h_head, u_head, ffn1_wb = _ffn(h, *ffn1, tf=FFN_HEAD_TF, rows=FFN_TM)
        h, u, _ = _ffn(h, ffn1[0], *ffn1_wb, *ffn1[4:], head=(h_head, u_head))

        z_act, f2_w1, f2_w2, wb_q, wb_k, wb_v, wb_xo = _proj(
            u, w_in_t, c0, c1 - c0, "silu", (),
            (ffn2_w1[i], ffn2_w2[i], w_q[i], w_k[i], w_v[i], w_xo[i]), seq=seq, layer=i, out_dtype=BF16)
        xbc = _proj(u, w_in_t, c1, c2 - c1, "conv_silu", (conv_w[i], vec(conv_b[i])), seq=seq, layer=i)
        dt2 = _proj(u, w_in_t, c2, c3 - c2, "softplus_bias", (twice(vec(dt_bias[i])),), seq=seq, layer=i,
                    repeat=2)
        gates, wb_ssd_out, wb_mix_out, wb_pool_out, wb_pool, f2_w3, wb_p = _proj(
            u, w_in_t, c3, 2 * D_MODEL, "sigmoid_bias", (vec(b_gate[i]),),
            (w_ssd_out[i], w_mix_out[i], w_pool_out[i], pool_w[i].reshape(-1, POOL_GROUP_DIM), ffn2_w3[i],
             (w_in_t[i], c0)),
            seq=seq, layer=i, out_dtype=BF16)

        gaya = _pool(u, wb_p, gates, wb_pool.reshape(pool_w.shape[1:]), vec(pool_scale[i]), wb_pool_out, seq)
        y = _ssd(xbc, dt2, twice(vec(a_log[i])), jnp.repeat(vec(d_skip[i]), SSD_HEAD_DIM, axis=1), batch)
        h, u = _merge(y, z_act, vec(ssd_norm_g[i]), wb_ssd_out, gaya, gates, wb_mix_out,
                      vec(mix_post_g[i]), h, vec(xattn_pre_g[i]))

        k, v = _kv(mem2, vec(mem_norm_g[i]), wb_k, wb_v)
        h = _attn(u, h, wb_q, k, v, wb_xo, vec(xattn_post_g[i]), seq)

        last = i == depth - 1
        h, u, _ = _ffn(h, vec(ffn2_pre_g[i]), f2_w1, f2_w3, f2_w2, 0.5 * vec(ffn2_post_g[i]),
                       None if last else vec(ffn1_pre_g[i + 1]))
    return h.reshape(batch, seq, d)
```
